```python
import jax
import jax.numpy as jnp
from jax import lax
import numpy as np

D_MODEL = 1024
BATCH = 2
SEQ = 8192
DEPTH = 4
DEC_BATCH = 32
DEC_SEQ = 8
PAST_LEN = 8192
PAGE_SIZE = 128

N_META = 16
SB_HEAD_DIM = 64
SB_WIDTH = D_MODEL // 2
N_SB_HEADS = SB_WIDTH // SB_HEAD_DIM
SB_BLOCK = 128
SB_BIAS_INIT = -8.0
CONV_CH = D_MODEL // 4
CONV_WIDTH = 31
GLA_HEADS = 4
GLA_WIDTH = D_MODEL // 4
GLA_DV = GLA_WIDTH // GLA_HEADS
GLA_DK = GLA_DV // 2
GLA_RANK = 16
GLA_GATE_NORM = 16.0
GLA_CHUNK = 64
MIX_WIDTH = SB_WIDTH + CONV_CH + GLA_WIDTH
EPS = 1e-6
IN_SPLITS = (SB_WIDTH, SB_WIDTH, SB_WIDTH, SB_WIDTH, 2 * CONV_CH, CONV_CH, GLA_HEADS * GLA_DK, GLA_HEADS * GLA_DK, GLA_WIDTH, GLA_WIDTH, GLA_RANK)
IN_WIDTH = sum(IN_SPLITS)

kernel_name = "hymba_stickbreak_conformer_gla_step"


def rms_norm(x, g):
    xf = x.astype(jnp.float32)
    y = xf * lax.rsqrt(jnp.mean(xf * xf, axis=-1, keepdims=True) + EPS) * g.astype(jnp.float32)
    return y.astype(x.dtype)


def split_in(u):
    parts = []
    start = 0
    for w in IN_SPLITS:
        parts.append(u[..., start:start + w])
        start += w
    return parts


def sb_attention(q, k, v, bias, q_pos0):
    b, tq, nh, dh = q.shape
    tk = k.shape[1]
    qb = min(SB_BLOCK, tq)
    nb = -(-tq // qb)
    qf = jnp.pad(q.astype(jnp.float32) * (dh ** -0.5), ((0, 0), (0, nb * qb - tq), (0, 0), (0, 0)))
    q_blocks = qf.reshape(b, nb, qb, nh, dh).swapaxes(0, 1)
    kf = k.astype(jnp.float32)
    vf = v.astype(jnp.float32)
    bf = bias.astype(jnp.float32)[None, :, None, None]
    k_pos = jnp.arange(tk)

    def one_block(args):
        q_blk, blk = args
        q_pos = q_pos0 + blk * qb + jnp.arange(qb)
        z = jnp.einsum('bqhd,bkhd->bhqk', q_blk, kf) + bf
        mask = k_pos[None, :] < q_pos[:, None]
        log_fail = jnp.where(mask, jax.nn.log_sigmoid(-z), 0.0)
        shifted = jnp.concatenate([log_fail[..., 1:], jnp.zeros_like(log_fail[..., :1])], axis=-1)
        log_w = jax.nn.log_sigmoid(z) + lax.cumsum(shifted, axis=3, reverse=True)
        w = jnp.where(mask, jnp.exp(log_w), 0.0)
        return jnp.einsum('bhqk,bkhd->bqhd', w, vf)

    out = lax.map(one_block, (q_blocks, jnp.arange(nb)))
    return out.swapaxes(0, 1).reshape(b, nb * qb, nh, dh)[:, :tq]


def causal_dwconv(u, state, w, bias):
    full = jnp.concatenate([state.astype(jnp.float32), u], axis=1)
    y = lax.conv_general_dilated(full, w.astype(jnp.float32)[:, None, :], window_strides=(1,), padding='VALID',
                                 dimension_numbers=('NWC', 'WIO', 'NWC'), feature_group_count=u.shape[-1])
    return y + bias.astype(jnp.float32), full[:, full.shape[1] - (CONV_WIDTH - 1):]


def gla_chunk(q, k, v, lg, s):
    n = q.shape[1]
    b = jnp.cumsum(lg, axis=1)
    o_inter = jnp.einsum('blhk,bhkv->blhv', q * jnp.exp(b), s)
    causal = jnp.tril(jnp.ones((n, n), dtype=bool))[None, :, :, None, None]
    diff = jnp.where(causal, b[:, :, None] - b[:, None, :], 0.0)
    decay = jnp.where(causal, jnp.exp(diff), 0.0)
    att = jnp.einsum('bthk,bshk,btshk->bhts', q, k, decay)
    o = o_inter + jnp.einsum('bhts,bshv->bthv', att, v)
    b_last = b[:, -1]
    k_dec = k * jnp.exp(b_last[:, None] - b)
    s_new = jnp.exp(b_last)[..., None] * s + jnp.einsum('blhk,blhv->bhkv', k_dec, v)
    return o, s_new


def gla_scan(q, k, v, lg, s):
    bsz, t = q.shape[0], q.shape[1]
    n = t // GLA_CHUNK

    def r(a):
        return a.reshape(bsz, n, GLA_CHUNK, *a.shape[2:]).swapaxes(0, 1)

    def body(state, xs):
        o, state = gla_chunk(xs[0], xs[1], xs[2], xs[3], state)
        return state, o

    s, o = lax.scan(body, s, (r(q), r(k), r(v), r(lg)))
    return o.swapaxes(0, 1).reshape(bsz, t, GLA_HEADS, GLA_DV), s


def gla_forward(q, k, v, lg, s, lead):
    s = s.astype(jnp.float32)
    outs = []
    if lead > 0:
        o, s = gla_chunk(q[:, :lead], k[:, :lead], v[:, :lead], lg[:, :lead], s)
        outs.append(o)
        q, k, v, lg = q[:, lead:], k[:, lead:], v[:, lead:], lg[:, lead:]
    t = q.shape[1]
    if t % GLA_CHUNK == 0 and t > GLA_CHUNK:
        o, s = gla_scan(q, k, v, lg, s)
    else:
        o, s = gla_chunk(q, k, v, lg, s)
    outs.append(o)
    return jnp.concatenate(outs, axis=1), s


def mixer_layer(x, norm_g, w_in, sb_bias, dw_w, dw_b, ln_g, ln_b, pw2, gk_w2, gk_b, gla_g, w_out,
                past_k, past_v, conv_state, gla_state, lead):
    f32 = jnp.float32
    bsz, t, _ = x.shape
    h = rms_norm(x, norm_g)
    u = h @ w_in
    sb_q, sb_k, sb_v, sb_gate, c_in, c_gate, g_q, g_k, g_v, g_gate, g_lr = split_in(u)

    q = sb_q.reshape(bsz, t, N_SB_HEADS, SB_HEAD_DIM)
    k = sb_k.reshape(bsz, t, N_SB_HEADS, SB_HEAD_DIM)
    v = sb_v.reshape(bsz, t, N_SB_HEADS, SB_HEAD_DIM)
    if past_k is None:
        k_all, v_all, q_pos0 = k, v, 0
    else:
        k_all = jnp.concatenate([past_k.astype(k.dtype), k], axis=1)
        v_all = jnp.concatenate([past_v.astype(v.dtype), v], axis=1)
        q_pos0 = past_k.shape[1]
    sb_o = sb_attention(q, k_all, v_all, sb_bias, q_pos0).reshape(bsz, t, SB_WIDTH)
    sb_o = sb_o * jax.nn.silu(sb_gate.astype(f32))

    a, g = jnp.split(c_in.astype(f32), 2, axis=-1)
    glu = a * jax.nn.sigmoid(g)
    c, conv_new = causal_dwconv(glu, conv_state, dw_w, dw_b)
    mu = jnp.mean(c, axis=-1, keepdims=True)
    var = jnp.mean(jnp.square(c - mu), axis=-1, keepdims=True)
    c = (c - mu) * lax.rsqrt(var + EPS) * ln_g.astype(f32) + ln_b.astype(f32)
    c = jax.nn.silu(c) @ pw2.astype(f32)
    conv_o = c * jax.nn.silu(c_gate.astype(f32))

    lg = jax.nn.log_sigmoid(g_lr.astype(f32) @ gk_w2.astype(f32) + gk_b.astype(f32)) / GLA_GATE_NORM
    gq = g_q.astype(f32).reshape(bsz, t, GLA_HEADS, GLA_DK) * (GLA_DK ** -0.5)
    gk = g_k.astype(f32).reshape(bsz, t, GLA_HEADS, GLA_DK)
    gv = g_v.astype(f32).reshape(bsz, t, GLA_HEADS, GLA_DV)
    lg = lg.reshape(bsz, t, GLA_HEADS, GLA_DK)
    o, s_new = gla_forward(gq, gk, gv, lg, gla_state, lead)
    o = o * lax.rsqrt(jnp.mean(o * o, axis=-1, keepdims=True) + EPS) * gla_g.astype(f32)
    gla_o = o.reshape(bsz, t, GLA_WIDTH) * jax.nn.silu(g_gate.astype(f32))

    mix = jnp.concatenate([sb_o, conv_o, gla_o], axis=-1).astype(x.dtype)
    y = x + mix @ w_out
    return y, k, v, conv_new.astype(x.dtype), s_new.astype(x.dtype)


def setup_inputs(seed: int = 0) -> dict:
    key = jax.random.key(seed)
    ks = jax.random.split(key, 21)
    n_pages = PAST_LEN // PAGE_SIZE
    n_used = DEC_BATCH * n_pages
    n_pool = n_used + (n_used + 3) // 4
    nrm = jax.random.normal
    page_table = jax.random.permutation(ks[6], n_pool)[:n_used].reshape(DEC_BATCH, n_pages).astype(jnp.int32)
    return {
        'x_prompt': nrm(ks[0], (BATCH, SEQ, D_MODEL), jnp.float32),
        'x_sample': nrm(ks[1], (DEC_BATCH, DEC_SEQ, D_MODEL), jnp.float32),
        'cache_k': nrm(ks[2], (DEPTH, n_pool, PAGE_SIZE, N_SB_HEADS, SB_HEAD_DIM), jnp.float32),
        'cache_v': nrm(ks[3], (DEPTH, n_pool, PAGE_SIZE, N_SB_HEADS, SB_HEAD_DIM), jnp.float32),
        'state_conv': 0.5 * nrm(ks[4], (DEPTH, DEC_BATCH, CONV_WIDTH - 1, CONV_CH), jnp.float32),
        'state_gla': nrm(ks[5], (DEPTH, DEC_BATCH, GLA_HEADS, GLA_DK, GLA_DV), jnp.float32),
        'page_table': page_table,
        'meta_tokens': nrm(ks[7], (N_META, D_MODEL), jnp.float32),
        'norm_g': 1.0 + 0.02 * nrm(ks[8], (DEPTH, D_MODEL), jnp.float32),
        'w_in': nrm(ks[9], (DEPTH, D_MODEL, IN_WIDTH), jnp.float32) * D_MODEL ** -0.5,
        'sb_bias': SB_BIAS_INIT + 0.5 * nrm(ks[20], (DEPTH, N_SB_HEADS), jnp.float32),
        'conv_dw_w': nrm(ks[10], (DEPTH, CONV_WIDTH, CONV_CH), jnp.float32) * CONV_WIDTH ** -0.5,
        'conv_dw_b': 0.02 * nrm(ks[11], (DEPTH, CONV_CH), jnp.float32),
        'conv_ln_g': 1.0 + 0.02 * nrm(ks[12], (DEPTH, CONV_CH), jnp.float32),
        'conv_ln_b': 0.02 * nrm(ks[13], (DEPTH, CONV_CH), jnp.float32),
        'conv_pw2': nrm(ks[14], (DEPTH, CONV_CH, CONV_CH), jnp.float32) * CONV_CH ** -0.5,
        'gla_gk_w2': nrm(ks[15], (DEPTH, GLA_RANK, GLA_HEADS * GLA_DK), jnp.float32) * GLA_RANK ** -0.5,
        'gla_gk_b': 0.1 * nrm(ks[16], (DEPTH, GLA_HEADS * GLA_DK), jnp.float32),
        'gla_norm_g': 1.0 + 0.02 * nrm(ks[17], (DEPTH, GLA_DV), jnp.float32),
        'w_out': nrm(ks[18], (DEPTH, MIX_WIDTH, D_MODEL), jnp.float32) * MIX_WIDTH ** -0.5,
        'final_norm_g': 1.0 + 0.02 * nrm(ks[19], (D_MODEL,), jnp.float32),
    }


def reference(x_prompt, x_sample, cache_k, cache_v, state_conv, state_gla, page_table, meta_tokens,
              norm_g, w_in, sb_bias, conv_dw_w, conv_dw_b, conv_ln_g, conv_ln_b, conv_pw2, gla_gk_w2, gla_gk_b,
              gla_norm_g, w_out, final_norm_g):
    bp = x_prompt.shape[0]
    bd = x_sample.shape[0]
    past_len = page_table.shape[1] * cache_k.shape[2]
    meta = jnp.broadcast_to(meta_tokens[None].astype(x_prompt.dtype), (bp, N_META, x_prompt.shape[-1]))
    xp = jnp.concatenate([meta, x_prompt], axis=1)
    xs = x_sample
    kp_l, vp_l, cp_l, sp_l = [], [], [], []
    ks_l, vs_l, cs_l, ss_l = [], [], [], []
    for l in range(DEPTH):
        w = (norm_g[l], w_in[l], sb_bias[l], conv_dw_w[l], conv_dw_b[l], conv_ln_g[l], conv_ln_b[l], conv_pw2[l],
             gla_gk_w2[l], gla_gk_b[l], gla_norm_g[l], w_out[l])
        xp, kp, vp, cp, sp = mixer_layer(xp, *w, None, None,
                                         jnp.zeros((bp, CONV_WIDTH - 1, CONV_CH), xp.dtype),
                                         jnp.zeros((bp, GLA_HEADS, GLA_DK, GLA_DV), xp.dtype), N_META)
        past_k = cache_k[l][page_table].reshape(bd, past_len, N_SB_HEADS, SB_HEAD_DIM)
        past_v = cache_v[l][page_table].reshape(bd, past_len, N_SB_HEADS, SB_HEAD_DIM)
        xs, k_s, v_s, c_s, s_s = mixer_layer(xs, *w, past_k, past_v, state_conv[l], state_gla[l], 0)
        kp_l.append(kp)
        vp_l.append(vp)
        cp_l.append(cp)
        sp_l.append(sp)
        ks_l.append(k_s)
        vs_l.append(v_s)
        cs_l.append(c_s)
        ss_l.append(s_s)
    y_prompt = rms_norm(xp, final_norm_g)[:, N_META:]
    y_sample = rms_norm(xs, final_norm_g)
    return (y_prompt, y_sample, jnp.stack(kp_l), jnp.stack(vp_l), jnp.stack(cp_l), jnp.stack(sp_l),
            jnp.stack(ks_l), jnp.stack(vs_l), jnp.stack(cs_l), jnp.stack(ss_l))
```

```python
import functools

import jax
import jax.numpy as jnp
from jax import lax
from jax.experimental import pallas as pl
from jax.experimental.pallas import tpu as pltpu

F32 = jnp.float32
BF16 = jnp.bfloat16

D_MODEL = 1024
N_META = 16
NH = 8
DH = 64
SBW = NH * DH
CCH = 256
CW = 31
GH = 4
GDK = 32
GDV = 64
GKW = GH * GDK
GVW = GH * GDV
GRANK = 16
GATE_NORM = 16.0
EPS = 1e-6

LANE = 128
BLK = 128
GSUB = 32
CPAD = 32
IN_W = 3600
IN_WP = 3712
OFF_Q, OFF_K, OFF_V, OFF_SBG = 0, 512, 1024, 1536
OFF_CIN, OFF_CG = 2048, 2560
OFF_GQ, OFF_GK, OFF_GV, OFF_GG, OFF_GLR = 2816, 2944, 3072, 3328, 3584
VMEM_LIMIT = 48 * 1024 * 1024


def _dot(a, b):
    return jnp.dot(a, b, preferred_element_type=F32)


def _dot_nt(a, b):
    return lax.dot_general(a, b, (((1,), (1,)), ((), ())), preferred_element_type=F32)


def _split2(x):
    hi = x.astype(BF16)
    lo = (x - hi.astype(F32)).astype(BF16)
    return hi, lo


def _split3(x):
    hi = x.astype(BF16)
    r = x - hi.astype(F32)
    mid = r.astype(BF16)
    lo = (r - mid.astype(F32)).astype(BF16)
    return hi, mid, lo


def _log_sigmoid(x):
    return jnp.minimum(x, 0.0) - jnp.log(1.0 + jnp.exp(-jnp.abs(x)))


def _silu(x):
    return x * jax.nn.sigmoid(x)


def _params(sem):
    return pltpu.CompilerParams(dimension_semantics=sem, vmem_limit_bytes=VMEM_LIMIT)


def _inproj_kernel(x_ref, g_ref, w_ref, u_ref, qkv_ref):
    x = x_ref[...]
    ms = jnp.mean(x * x, axis=-1, keepdims=True)
    h = (x * lax.rsqrt(ms + EPS) * g_ref[...]).astype(BF16)
    for c0 in range(0, IN_WP, SBW):
        wd = min(SBW, IN_WP - c0)
        r = _dot(h, w_ref[:, c0:c0 + wd])
        u_ref[:, c0:c0 + wd] = r
        if c0 == OFF_Q:
            qkv_ref[:, c0:c0 + wd] = (r * (DH ** -0.5)).astype(BF16)
        elif c0 < OFF_SBG:
            qkv_ref[:, c0:c0 + wd] = r.astype(BF16)


def _inproj(x, norm_g, w_in_bf, layer, bm):
    n = x.shape[0]
    return pl.pallas_call(
        _inproj_kernel,
        grid=(n // bm,),
        in_specs=[
            pl.BlockSpec((bm, D_MODEL), lambda i: (i, 0)),
            pl.BlockSpec((None, 1, D_MODEL), lambda i: (layer, 0, 0)),
            pl.BlockSpec((None, D_MODEL, IN_WP), lambda i: (layer, 0, 0)),
        ],
        out_specs=[
            pl.BlockSpec((bm, IN_WP), lambda i: (i, 0)),
            pl.BlockSpec((bm, 3 * SBW), lambda i: (i, 0)),
        ],
        out_shape=[
            jax.ShapeDtypeStruct((n, IN_WP), F32),
            jax.ShapeDtypeStruct((n, 3 * SBW), BF16),
        ],
        compiler_params=_params(("parallel",)),
        name="inproj",
    )(x, norm_g, w_in_bf)


def _outproj_kernel(x_ref, a_ref, b_ref, c_ref, w_ref, *rest, final):
    y = x_ref[...]
    y = y + _dot(a_ref[...].astype(BF16), w_ref[0:SBW, :])
    y = y + _dot(b_ref[...].astype(BF16), w_ref[SBW:SBW + CCH, :])
    y = y + _dot(c_ref[...].astype(BF16), w_ref[SBW + CCH:, :])
    if final:
        g_ref, y_ref = rest
        ms = jnp.mean(y * y, axis=-1, keepdims=True)
        y_ref[...] = y * lax.rsqrt(ms + EPS) * g_ref[...]
    else:
        (y_ref,) = rest
        y_ref[...] = y


def _outproj(x, mix_sb, mix_cv, mix_gla, w_out_bf, layer, bm, final_g=None):
    n = x.shape[0]
    final = final_g is not None
    in_specs = [
        pl.BlockSpec((bm, D_MODEL), lambda i: (i, 0)),
        pl.BlockSpec((bm, SBW), lambda i: (i, 0)),
        pl.BlockSpec((bm, CCH), lambda i: (i, 0)),
        pl.BlockSpec((bm, GVW), lambda i: (i, 0)),
        pl.BlockSpec((None, D_MODEL, D_MODEL), lambda i: (layer, 0, 0)),
    ]
    args = [x, mix_sb, mix_cv, mix_gla, w_out_bf]
    if final:
        in_specs.append(pl.BlockSpec((1, D_MODEL), lambda i: (0, 0)))
        args.append(final_g)
    return pl.pallas_call(
        functools.partial(_outproj_kernel, final=final),
        grid=(n // bm,),
        in_specs=in_specs,
        out_specs=pl.BlockSpec((bm, D_MODEL), lambda i: (i, 0)),
        out_shape=jax.ShapeDtypeStruct((n, D_MODEL), F32),
        compiler_params=_params(("parallel",)),
        name="outproj",
    )(*args)


def _sb_block(qs, kb, vb, bias, uo, carry_ref, acc_ref, mask, transposed=False):
    n = uo.shape[0]
    z = (_dot(qs, kb) if transposed else _dot_nt(qs, kb)) + bias
    ls = jnp.minimum(z, 0.0) - jnp.log(1.0 + jnp.exp(-jnp.abs(z)))
    lf = ls - z
    if mask is not None:
        lf = jnp.where(mask, lf, 0.0)
    cr = _dot(lf.astype(BF16), uo)
    w = jnp.exp(ls + cr[:, :n] + carry_ref[...])
    if mask is not None:
        w = jnp.where(mask, w, 0.0)
    w = w.astype(BF16)
    acc_ref[...] += _dot_nt(w, vb) if transposed else _dot(w, vb)
    carry_ref[...] += cr[:, n:]


def _attn_prompt_kernel(q_ref, k_ref, v_ref, gate_ref, bias_ref, uo_ref, o_ref, acc_ref, carry_ref):
    qi = pl.program_id(2)
    q = q_ref[...]
    lane = lax.broadcasted_iota(jnp.int32, q.shape, 1)
    zero = jnp.zeros_like(q)
    qs = jnp.concatenate([jnp.where(lane < DH, q, zero), jnp.where(lane >= DH, q, zero)], axis=0)
    bias = bias_ref[...]
    uo = uo_ref[...]
    acc_ref[...] = jnp.zeros_like(acc_ref)
    carry_ref[...] = jnp.zeros_like(carry_ref)
    row = lax.broadcasted_iota(jnp.int32, (2 * BLK, BLK), 0) & (BLK - 1)
    col = lax.broadcasted_iota(jnp.int32, (2 * BLK, BLK), 1)
    d0 = pl.multiple_of(qi * BLK, BLK)
    _sb_block(qs, k_ref[pl.ds(d0, BLK), :], v_ref[pl.ds(d0, BLK), :], bias, uo, carry_ref, acc_ref, col < row)

    def body(j, c):
        s = pl.multiple_of((qi - 1 - j) * BLK, BLK)
        _sb_block(qs, k_ref[pl.ds(s, BLK), :], v_ref[pl.ds(s, BLK), :], bias, uo, carry_ref, acc_ref, None)
        return c

    lax.fori_loop(0, qi, body, 0)
    acc = acc_ref[...]
    o = jnp.where(lane < DH, acc[:BLK], acc[BLK:])
    o_ref[...] = (o * _silu(gate_ref[...])).astype(o_ref.dtype)


def _attn_prompt(qkv, u, bias_pairs, uo, nseq, tpad):
    n = nseq * tpad
    nq = tpad // BLK
    hp = NH // 2
    kcol, vcol, gcol = OFF_K // LANE, OFF_V // LANE, OFF_SBG // LANE
    return pl.pallas_call(
        _attn_prompt_kernel,
        grid=(nseq, hp, nq),
        in_specs=[
            pl.BlockSpec((BLK, LANE), lambda b, h, i: (b * nq + i, h)),
            pl.BlockSpec((tpad, LANE), lambda b, h, i: (b, kcol + h)),
            pl.BlockSpec((tpad, LANE), lambda b, h, i: (b, vcol + h)),
            pl.BlockSpec((BLK, LANE), lambda b, h, i: (b * nq + i, gcol + h)),
            pl.BlockSpec((None, 2 * BLK, BLK), lambda b, h, i: (h, 0, 0)),
            pl.BlockSpec((BLK, 2 * BLK), lambda b, h, i: (0, 0)),
        ],
        out_specs=pl.BlockSpec((BLK, LANE), lambda b, h, i: (b * nq + i, h)),
        out_shape=jax.ShapeDtypeStruct((n, SBW), BF16),
        scratch_shapes=[pltpu.VMEM((2 * BLK, LANE), F32), pltpu.VMEM((2 * BLK, BLK), F32)],
        compiler_params=_params(("arbitrary", "arbitrary", "arbitrary")),
        name="attn_prompt",
    )(qkv, qkv, qkv, u, bias_pairs, uo)


def _attn_sample_kernel(pt_ref, q_ref, kn_ref, vn_ref, gate_ref, kp_ref, vp_ref, bias_ref, uo_ref,
                        o_ref, qs_ref, acc_ref, carry_ref, *, n_pages):
    del pt_ref
    j = pl.program_id(1)
    s = q_ref.shape[0]
    m = NH * s
    page = kp_ref.shape[1]

    @pl.when(j == 0)
    def _():
        q = q_ref[...] * (DH ** -0.5)
        head = lax.broadcasted_iota(jnp.int32, (s, SBW), 1) // DH
        qs = jnp.concatenate([jnp.where(head == h, q, 0.0) for h in range(NH)], axis=0).astype(BF16)
        qs_ref[...] = qs
        acc_ref[...] = jnp.zeros_like(acc_ref)
        carry_ref[...] = jnp.zeros_like(carry_ref)
        pad = jnp.zeros((page - s, SBW), F32)
        kb = jnp.concatenate([kn_ref[...], pad], axis=0).astype(BF16)
        vb = jnp.concatenate([vn_ref[...], pad], axis=0).astype(BF16)
        row = lax.rem(lax.broadcasted_iota(jnp.int32, (m, page), 0), s)
        col = lax.broadcasted_iota(jnp.int32, (m, page), 1)
        _sb_block(qs, kb, vb, bias_ref[...], uo_ref[...], carry_ref, acc_ref, col < row)

    @pl.when(j > 0)
    def _():
        _sb_block(qs_ref[...], kp_ref[...].astype(BF16), vp_ref[...].astype(BF16), bias_ref[...], uo_ref[...],
                  carry_ref, acc_ref, None, transposed=True)

    @pl.when(j == n_pages)
    def _():
        acc = acc_ref[...]
        head = lax.broadcasted_iota(jnp.int32, (s, SBW), 1) // DH
        o = jnp.zeros((s, SBW), F32)
        for h in range(NH):
            o = jnp.where(head == h, acc[h * s:(h + 1) * s], o)
        o_ref[...] = o * _silu(gate_ref[...])


def _attn_sample(u, cache_k, cache_v, page_table, bias_rows, uo, layer, nseq, s):
    n_pages = page_table.shape[1]
    page = cache_k.shape[3]

    def page_map(b, j, pt):
        return (layer, pt[b, n_pages - jnp.maximum(j, 1)], 0, 0)

    grid_spec = pltpu.PrefetchScalarGridSpec(
        num_scalar_prefetch=1,
        grid=(nseq, n_pages + 1),
        in_specs=[
            pl.BlockSpec((s, SBW), lambda b, j, pt: (b, OFF_Q // SBW)),
            pl.BlockSpec((s, SBW), lambda b, j, pt: (b, OFF_K // SBW)),
            pl.BlockSpec((s, SBW), lambda b, j, pt: (b, OFF_V // SBW)),
            pl.BlockSpec((s, SBW), lambda b, j, pt: (b, OFF_SBG // SBW)),
            pl.BlockSpec((None, None, SBW, page), page_map),
            pl.BlockSpec((None, None, SBW, page), page_map),
            pl.BlockSpec((NH * s, page), lambda b, j, pt: (0, 0)),
            pl.BlockSpec((page, 2 * page), lambda b, j, pt: (0, 0)),
        ],
        out_specs=pl.BlockSpec((s, SBW), lambda b, j, pt: (b, 0)),
        scratch_shapes=[pltpu.VMEM((NH * s, SBW), BF16), pltpu.VMEM((NH * s, SBW), F32),
                        pltpu.VMEM((NH * s, page), F32)],
    )
    return pl.pallas_call(
        functools.partial(_attn_sample_kernel, n_pages=n_pages),
        grid_spec=grid_spec,
        out_shape=jax.ShapeDtypeStruct((nseq * s, SBW), F32),
        compiler_params=_params(("arbitrary", "arbitrary")),
        name="attn_sample",
    )(page_table, u, u, u, u, cache_k, cache_v, bias_rows, uo)


def _conv_kernel(cin_ref, cg_ref, st_ref, w_ref, b_ref, lng_ref, lnb_ref, pw_ref, o_ref, stout_ref,
                 full_ref, y_ref, *, tb, sub, last_t, state_off):
    t = pl.program_id(1)

    @pl.when(t == 0)
    def _():
        full_ref[0:CPAD, :] = st_ref[...]

    cin = cin_ref[...]
    full_ref[CPAD:CPAD + tb, :] = cin[:, :CCH] * jax.nn.sigmoid(cin[:, CCH:])
    lead = CPAD - (CW - 1)
    for r0 in range(0, tb, sub):
        acc = jnp.zeros((sub, CCH), F32) + b_ref[...]
        for j in range(CW):
            acc = acc + w_ref[j:j + 1, :] * full_ref[r0 + lead + j:r0 + lead + j + sub, :]
        y_ref[r0:r0 + sub, :] = acc
    c = y_ref[...]
    mu = jnp.mean(c, axis=-1, keepdims=True)
    d = c - mu
    var = jnp.mean(d * d, axis=-1, keepdims=True)
    c = d * lax.rsqrt(var + EPS) * lng_ref[...] + lnb_ref[...]
    c = _dot(_silu(c).astype(BF16), pw_ref[...])
    o_ref[...] = (c * _silu(cg_ref[...])).astype(o_ref.dtype)

    @pl.when(t == last_t)
    def _():
        stout_ref[...] = full_ref[state_off:state_off + CPAD, :]

    full_ref[0:CPAD, :] = full_ref[tb:tb + CPAD, :]


def _conv(u, state_pad, dw_w, dw_b, ln_g, ln_b, pw2_bf, layer, nseq, tpad, t_real, tb, out_dtype):
    nt = tpad // tb
    last_t = (t_real - 1) // tb
    state_off = t_real - last_t * tb
    sub = min(tb, 32)
    kern = functools.partial(_conv_kernel, tb=tb, sub=sub, last_t=last_t, state_off=state_off)
    vec = pl.BlockSpec((None, 1, CCH), lambda b, t: (layer, 0, 0))
    return pl.pallas_call(
        kern,
        grid=(nseq, nt),
        in_specs=[
            pl.BlockSpec((tb, 2 * CCH), lambda b, t: (b * nt + t, OFF_CIN // (2 * CCH))),
            pl.BlockSpec((tb, CCH), lambda b, t: (b * nt + t, OFF_CG // CCH)),
            pl.BlockSpec((None, CPAD, CCH), lambda b, t: (b, 0, 0)),
            pl.BlockSpec((None, CPAD, CCH), lambda b, t: (layer, 0, 0)),
            vec, vec, vec,
            pl.BlockSpec((None, CCH, CCH), lambda b, t: (layer, 0, 0)),
        ],
        out_specs=[
            pl.BlockSpec((tb, CCH), lambda b, t: (b * nt + t, 0)),
            pl.BlockSpec((None, CPAD, CCH), lambda b, t: (b, 0, 0)),
        ],
        out_shape=[
            jax.ShapeDtypeStruct((nseq * tpad, CCH), out_dtype),
            jax.ShapeDtypeStruct((nseq, CPAD, CCH), F32),
        ],
        scratch_shapes=[pltpu.VMEM((CPAD + tb, CCH), F32), pltpu.VMEM((tb, CCH), F32)],
        compiler_params=_params(("arbitrary", "arbitrary")),
        name="conv",
    )(u, u, state_pad, dw_w, dw_b, ln_g, ln_b, pw2_bf)


def _gla_kernel(gq_ref, gk_ref, gv_ref, gg_ref, glr_ref, s0_ref, w2h_ref, w2l_ref, gkb_ref, ng_ref,
                ts_ref, gmat_ref, o_ref, sout_ref, s_ref, *, rows, t_real, n_chunks):
    c = pl.program_id(1)

    @pl.when(c == 0)
    def _():
        s_ref[...] = s0_ref[...]

    def padrows(x):
        if rows == BLK:
            return x
        return jnp.concatenate([x, jnp.zeros((BLK - rows, x.shape[1]), x.dtype)], axis=0)

    ridx = lax.broadcasted_iota(jnp.int32, (BLK, GKW), 0)
    valid = ridx < jnp.minimum(rows, t_real - c * rows)
    gq = padrows(gq_ref[...]) * (GDK ** -0.5)
    gk = jnp.where(valid, padrows(gk_ref[...]), 0.0)
    gv = padrows(gv_ref[...])
    glr_h, glr_l = _split2(padrows(glr_ref[...]))
    x = _dot(glr_h, w2h_ref[...]) + _dot(glr_l, w2h_ref[...]) + _dot(glr_h, w2l_ref[...]) + gkb_ref[...]
    lg = jnp.where(valid, _log_sigmoid(x) * (1.0 / GATE_NORM), 0.0)
    ts = ts_ref[...]
    lg_h, lg_m, lg_l = _split3(lg)
    br = _dot(ts, lg_h) + _dot(ts, lg_m) + _dot(ts, lg_l)
    b = br[:BLK]
    r = br[BLK:]
    s_bd = s_ref[...]
    o_inter = _dot((gq * jnp.exp(b)).astype(BF16), s_bd.astype(BF16))
    qd = gq * jnp.exp(b - r)
    gv_bf = gv.astype(BF16)
    nst = GH * GSUB
    qmask = (lax.broadcasted_iota(jnp.int32, (nst, GKW), 1) // GDK) == (lax.broadcasted_iota(jnp.int32, (nst, GKW), 0) // GSUB)
    vmask = (lax.broadcasted_iota(jnp.int32, (nst, GVW), 1) // GDV) == (lax.broadcasted_iota(jnp.int32, (nst, GVW), 0) // GSUB)
    trow = lax.rem(lax.broadcasted_iota(jnp.int32, (nst, BLK), 0), GSUB)
    scol = lax.broadcasted_iota(jnp.int32, (nst, BLK), 1)
    outs = []
    for i in range(BLK // GSUB):
        lo = i * GSUB
        kd = (gk * jnp.exp(r[lo:lo + 1, :] - b)).astype(BF16)
        qd_i = qd[lo:lo + GSUB]
        qst = jnp.where(qmask, jnp.concatenate([qd_i] * GH, axis=0), 0.0).astype(BF16)
        att = jnp.where(scol <= trow + lo, _dot_nt(qst, kd), 0.0)
        ov = jnp.where(vmask, _dot(att.astype(BF16), gv_bf), 0.0)
        o_i = ov[0:GSUB]
        for h in range(1, GH):
            o_i = o_i + ov[h * GSUB:(h + 1) * GSUB]
        outs.append(o_i)
    o = jnp.concatenate(outs, axis=0) + o_inter
    o2_h, o2_l = _split2(o * o)
    ms = _dot(o2_h, gmat_ref[...]) + _dot(o2_l, gmat_ref[...])
    res = o * lax.rsqrt(ms + EPS) * ng_ref[...] * _silu(padrows(gg_ref[...]))
    o_ref[...] = res[:rows].astype(o_ref.dtype)
    kdec = gk * jnp.exp(b[BLK - 1:BLK, :] - b)
    kv = _dot(kdec.T.astype(BF16), gv_bf)
    bl_col = jnp.sum(lg.T, axis=1, keepdims=True)
    bdmask = (lax.broadcasted_iota(jnp.int32, (GKW, GVW), 0) // GDK) == (lax.broadcasted_iota(jnp.int32, (GKW, GVW), 1) // GDV)
    s_new = jnp.exp(bl_col) * s_bd + jnp.where(bdmask, kv, 0.0)
    s_ref[...] = s_new

    @pl.when(c == n_chunks - 1)
    def _():
        sout_ref[...] = s_new


def _gla(u, s0_bd, w2h, w2l, gkb, ng_tiled, ts, gmat, layer, nseq, tpad, t_real, rows, out_dtype):
    nc = tpad // rows
    kern = functools.partial(_gla_kernel, rows=rows, t_real=t_real, n_chunks=nc)
    return pl.pallas_call(
        kern,
        grid=(nseq, nc),
        in_specs=[
            pl.BlockSpec((rows, GKW), lambda b, c: (b * nc + c, OFF_GQ // GKW)),
            pl.BlockSpec((rows, GKW), lambda b, c: (b * nc + c, OFF_GK // GKW)),
            pl.BlockSpec((rows, GVW), lambda b, c: (b * nc + c, OFF_GV // GVW)),
            pl.BlockSpec((rows, GVW), lambda b, c: (b * nc + c, OFF_GG // GVW)),
            pl.BlockSpec((rows, LANE), lambda b, c: (b * nc + c, OFF_GLR // LANE)),
            pl.BlockSpec((None, GKW, GVW), lambda b, c: (b, 0, 0)),
            pl.BlockSpec((None, LANE, GKW), lambda b, c: (layer, 0, 0)),
            pl.BlockSpec((None, LANE, GKW), lambda b, c: (layer, 0, 0)),
            pl.BlockSpec((None, 1, GKW), lambda b, c: (layer, 0, 0)),
            pl.BlockSpec((None, 1, GVW), lambda b, c: (layer, 0, 0)),
            pl.BlockSpec((2 * BLK, BLK), lambda b, c: (0, 0)),
            pl.BlockSpec((GVW, GVW), lambda b, c: (0, 0)),
        ],
        out_specs=[
            pl.BlockSpec((rows, GVW), lambda b, c: (b * nc + c, 0)),
            pl.BlockSpec((None, GKW, GVW), lambda b, c: (b, 0, 0)),
        ],
        out_shape=[
            jax.ShapeDtypeStruct((nseq * tpad, GVW), out_dtype),
            jax.ShapeDtypeStruct((nseq, GKW, GVW), F32),
        ],
        scratch_shapes=[pltpu.VMEM((GKW, GVW), F32)],
        compiler_params=_params(("arbitrary", "arbitrary")),
        name="gla",
    )(u, u, u, u, u, s0_bd, w2h, w2l, gkb, ng_tiled, ts, gmat)


def _state_to_blockdiag(s):
    n = s.shape[0]
    out = jnp.zeros((n, GH, GDK, GH, GDV), F32)
    for h in range(GH):
        out = out.at[:, h, :, h, :].set(s[:, h].astype(F32))
    return out.reshape(n, GKW, GVW)


def _blockdiag_to_state(s_bd):
    n = s_bd.shape[0]
    s5 = s_bd.reshape(n, GH, GDK, GH, GDV)
    return jnp.stack([s5[:, h, :, h, :] for h in range(GH)], axis=1)


def _row_block(n):
    for bm in (256, 128, 64, 32, 16, 8):
        if n % bm == 0:
            return bm
    raise ValueError(f"row count {n} is not a multiple of 8")


def kernel(x_prompt, x_sample, cache_k, cache_v, state_conv, state_gla, page_table, meta_tokens, norm_g, w_in,
           sb_bias, conv_dw_w, conv_dw_b, conv_ln_g, conv_ln_b, conv_pw2, gla_gk_w2, gla_gk_b, gla_norm_g, w_out,
           final_norm_g):
    bp, seq, _ = x_prompt.shape
    bd, dseq, _ = x_sample.shape
    depth = w_in.shape[0]
    n_pool, page = cache_k.shape[1], cache_k.shape[2]
    assert page == BLK and dseq % 8 == 0 and BLK % dseq == 0
    t_real = N_META + seq
    tpad = -(-t_real // BLK) * BLK

    w_in_bf = jnp.pad(w_in, ((0, 0), (0, 0), (0, IN_WP - IN_W))).astype(BF16)
    w_out_bf = w_out.astype(BF16)
    norm_g3 = norm_g.reshape(depth, 1, D_MODEL)
    dw_w = jnp.pad(conv_dw_w, ((0, 0), (0, CPAD - CW), (0, 0)))
    dw_b = conv_dw_b.reshape(depth, 1, CCH)
    ln_g = conv_ln_g.reshape(depth, 1, CCH)
    ln_b = conv_ln_b.reshape(depth, 1, CCH)
    pw2_bf = conv_pw2.astype(BF16)
    w2 = jnp.pad(gla_gk_w2, ((0, 0), (0, LANE - GRANK), (0, 0)))
    w2h = w2.astype(BF16)
    w2l = (w2 - w2h.astype(F32)).astype(BF16)
    gkb = gla_gk_b.reshape(depth, 1, GKW)
    ng_tiled = jnp.tile(gla_norm_g, (1, GH)).reshape(depth, 1, GVW)
    ii = jnp.arange(BLK)
    uo = jnp.concatenate([(ii[:, None] > ii[None, :]), jnp.ones((BLK, BLK), bool)], axis=1).astype(BF16)
    tri = ii[None, :] <= ii[:, None]
    sel = ii[None, :] < (ii[:, None] // GSUB) * GSUB
    ts = jnp.concatenate([tri, sel], axis=0).astype(BF16)
    gi = jnp.arange(GVW) // GDV
    gmat = ((gi[:, None] == gi[None, :]).astype(F32) / GDV).astype(BF16)
    final_g = final_norm_g.reshape(1, D_MODEL)

    meta = jnp.broadcast_to(meta_tokens[None].astype(F32), (bp, N_META, D_MODEL))
    xp = jnp.concatenate([meta, x_prompt, jnp.zeros((bp, tpad - t_real, D_MODEL), F32)], axis=1)
    xp = xp.reshape(bp * tpad, D_MODEL)
    xs = x_sample.reshape(bd * dseq, D_MODEL)
    bm_p = _row_block(bp * tpad)
    bm_s = _row_block(bd * dseq)
    conv_tb = BLK
    cache_k4 = jnp.transpose(cache_k, (0, 1, 3, 4, 2)).reshape(depth, n_pool, SBW, page)
    cache_v4 = jnp.transpose(cache_v, (0, 1, 3, 4, 2)).reshape(depth, n_pool, SBW, page)
    zero_conv = jnp.zeros((bp, CPAD, CCH), F32)
    zero_gla = jnp.zeros((bp, GKW, GVW), F32)

    kp_l, vp_l, cp_l, sp_l, ks_l, vs_l, cs_l, ss_l = ([] for _ in range(8))
    for l in range(depth):
        bias_pairs = jnp.broadcast_to(
            jnp.repeat(sb_bias[l].reshape(NH // 2, 2), BLK, axis=1)[:, :, None], (NH // 2, 2 * BLK, BLK))
        bias_rows = jnp.broadcast_to(jnp.repeat(sb_bias[l], dseq)[:, None], (NH * dseq, page))
        last = l == depth - 1
        u_p, qkv_p = _inproj(xp, norm_g3, w_in_bf, l, bm_p)
        mix_sb = _attn_prompt(qkv_p, u_p, bias_pairs, uo, bp, tpad)
        mix_cv, cst_p = _conv(u_p, zero_conv, dw_w, dw_b, ln_g, ln_b, pw2_bf, l, bp, tpad, t_real, conv_tb, BF16)
        mix_gla, gst_p = _gla(u_p, zero_gla, w2h, w2l, gkb, ng_tiled, ts, gmat, l, bp, tpad, t_real, BLK, BF16)
        xp = _outproj(xp, mix_sb, mix_cv, mix_gla, w_out_bf, l, bm_p, final_g if last else None)
        kv = u_p[:, OFF_K:OFF_SBG].reshape(bp, tpad, 2, NH, DH)[:, :t_real]
        kp_l.append(kv[:, :, 0])
        vp_l.append(kv[:, :, 1])
        cp_l.append(cst_p[:, CPAD - (CW - 1):])
        sp_l.append(_blockdiag_to_state(gst_p))
        u_s, _ = _inproj(xs, norm_g3, w_in_bf, l, bm_s)
        smix_sb = _attn_sample(u_s, cache_k4, cache_v4, page_table, bias_rows, uo, l, bd, dseq)
        st_s = jnp.pad(state_conv[l], ((0, 0), (CPAD - (CW - 1), 0), (0, 0)))
        smix_cv, cst_s = _conv(u_s, st_s, dw_w, dw_b, ln_g, ln_b, pw2_bf, l, bd, dseq, dseq, dseq, F32)
        smix_gla, gst_s = _gla(u_s, _state_to_blockdiag(state_gla[l]), w2h, w2l, gkb, ng_tiled, ts, gmat, l, bd,
                               dseq, dseq, dseq, F32)
        xs = _outproj(xs, smix_sb, smix_cv, smix_gla, w_out_bf, l, bm_s, final_g if last else None)
        ks_l.append(u_s[:, OFF_K:OFF_V].reshape(bd, dseq, NH, DH))
        vs_l.append(u_s[:, OFF_V:OFF_SBG].reshape(bd, dseq, NH, DH))
        cs_l.append(cst_s[:, CPAD - (CW - 1):])
        ss_l.append(_blockdiag_to_state(gst_s))

    y_prompt = xp.reshape(bp, tpad, D_MODEL)[:, N_META:t_real]
    y_sample = xs.reshape(bd, dseq, D_MODEL)
    return (y_prompt, y_sample, jnp.stack(kp_l), jnp.stack(vp_l), jnp.stack(cp_l), jnp.stack(sp_l),
            jnp.stack(ks_l), jnp.stack(vs_l), jnp.stack(cs_l), jnp.stack(ss_l))
```

```python
import functools

import jax
import jax.numpy as jnp
from jax import lax
from jax.experimental import pallas as pl
from jax.experimental.pallas import tpu as pltpu

F32 = jnp.float32
BF16 = jnp.bfloat16

D_MODEL = 1024
N_META = 16
NH = 8
DH = 64
SBW = NH * DH
CCH = 256
CW = 31
GH = 4
GDK = 32
GDV = 64
GKW = GH * GDK
GVW = GH * GDV
GRANK = 16
GATE_NORM = 16.0
EPS = 1e-6

LANE = 128
MXU_DIM = 256
BLK = 128
KBLK = MXU_DIM
LOG2E = 1.4426950408889634
Q_SCALE = DH ** -0.5 * LOG2E
GSUB = 32
CPAD = 32
IN_W = 3600
IN_WP = 3712
OFF_Q, OFF_K, OFF_V, OFF_SBG = 0, 512, 1024, 1536
OFF_CIN, OFF_CG = 2048, 2560
OFF_GQ, OFF_GK, OFF_GV, OFF_GG, OFF_GLR = 2816, 2944, 3072, 3328, 3584
VMEM_LIMIT = 48 * 1024 * 1024


def _dot(a, b):
    return jnp.dot(a, b, preferred_element_type=F32)


def _dot_nt(a, b):
    return lax.dot_general(a, b, (((1,), (1,)), ((), ())), preferred_element_type=F32)


def _split2(x):
    hi = x.astype(BF16)
    lo = (x - hi.astype(F32)).astype(BF16)
    return hi, lo


def _split3(x):
    hi = x.astype(BF16)
    r = x - hi.astype(F32)
    mid = r.astype(BF16)
    lo = (r - mid.astype(F32)).astype(BF16)
    return hi, mid, lo


def _log_sigmoid(x):
    return jnp.minimum(x, 0.0) - jnp.log(1.0 + jnp.exp(-jnp.abs(x)))


def _silu(x):
    return x * jax.nn.sigmoid(x)


def _params(sem):
    return pltpu.CompilerParams(dimension_semantics=sem, vmem_limit_bytes=VMEM_LIMIT)


def _inproj_kernel(x_ref, g_ref, w_ref, u_ref, qkv_ref):
    x = x_ref[...]
    ms = jnp.mean(x * x, axis=-1, keepdims=True)
    h = (x * lax.rsqrt(ms + EPS) * g_ref[...]).astype(BF16)
    for c0 in range(0, IN_WP, SBW):
        wd = min(SBW, IN_WP - c0)
        r = _dot(h, w_ref[:, c0:c0 + wd])
        u_ref[:, c0:c0 + wd] = r
        if c0 == OFF_Q:
            qkv_ref[:, c0:c0 + wd] = (r * Q_SCALE).astype(BF16)
        elif c0 < OFF_SBG:
            qkv_ref[:, c0:c0 + wd] = r.astype(BF16)


def _inproj(x, norm_g, w_in_bf, layer, bm):
    n = x.shape[0]
    return pl.pallas_call(
        _inproj_kernel,
        grid=(n // bm,),
        in_specs=[
            pl.BlockSpec((bm, D_MODEL), lambda i: (i, 0)),
            pl.BlockSpec((None, 1, D_MODEL), lambda i: (layer, 0, 0)),
            pl.BlockSpec((None, D_MODEL, IN_WP), lambda i: (layer, 0, 0)),
        ],
        out_specs=[
            pl.BlockSpec((bm, IN_WP), lambda i: (i, 0)),
            pl.BlockSpec((bm, 3 * SBW), lambda i: (i, 0)),
        ],
        out_shape=[
            jax.ShapeDtypeStruct((n, IN_WP), F32),
            jax.ShapeDtypeStruct((n, 3 * SBW), BF16),
        ],
        compiler_params=_params(("parallel",)),
        name="inproj",
    )(x, norm_g, w_in_bf)


def _outproj_kernel(x_ref, a_ref, b_ref, c_ref, w_ref, *rest, final):
    y = x_ref[...]
    y = y + _dot(a_ref[...].astype(BF16), w_ref[0:SBW, :])
    y = y + _dot(b_ref[...].astype(BF16), w_ref[SBW:SBW + CCH, :])
    y = y + _dot(c_ref[...].astype(BF16), w_ref[SBW + CCH:, :])
    if final:
        g_ref, y_ref = rest
        ms = jnp.mean(y * y, axis=-1, keepdims=True)
        y_ref[...] = y * lax.rsqrt(ms + EPS) * g_ref[...]
    else:
        (y_ref,) = rest
        y_ref[...] = y


def _outproj(x, mix_sb, mix_cv, mix_gla, w_out_bf, layer, bm, final_g=None):
    n = x.shape[0]
    final = final_g is not None
    in_specs = [
        pl.BlockSpec((bm, D_MODEL), lambda i: (i, 0)),
        pl.BlockSpec((bm, SBW), lambda i: (i, 0)),
        pl.BlockSpec((bm, CCH), lambda i: (i, 0)),
        pl.BlockSpec((bm, GVW), lambda i: (i, 0)),
        pl.BlockSpec((None, D_MODEL, D_MODEL), lambda i: (layer, 0, 0)),
    ]
    args = [x, mix_sb, mix_cv, mix_gla, w_out_bf]
    if final:
        in_specs.append(pl.BlockSpec((1, D_MODEL), lambda i: (0, 0)))
        args.append(final_g)
    return pl.pallas_call(
        functools.partial(_outproj_kernel, final=final),
        grid=(n // bm,),
        in_specs=in_specs,
        out_specs=pl.BlockSpec((bm, D_MODEL), lambda i: (i, 0)),
        out_shape=jax.ShapeDtypeStruct((n, D_MODEL), F32),
        compiler_params=_params(("parallel",)),
        name="outproj",
    )(*args)


def _neg_abs(x):
    bits = lax.bitcast_convert_type(x, jnp.uint32) | jnp.uint32(0x80000000)
    return lax.bitcast_convert_type(bits, F32)


def _sb_logs(z):
    l = jnp.log(1.0 + jnp.exp2(_neg_abs(z))) * LOG2E
    ls = jnp.minimum(z, 0.0) - l
    return ls, ls - z


def _lane_tile(x, n):
    reps = n // x.shape[1]
    return x if reps == 1 else jnp.concatenate([x] * reps, axis=1)


def _sb_weights(ls, cr, carry, mask):
    n = ls.shape[1]
    w = jnp.exp2(ls + cr[:, :n] + _lane_tile(carry, n))
    if mask is not None:
        w = jnp.where(mask, w, 0.0)
    return w.astype(BF16)


def _sb_block(qs, kb, vb, bias, uo, carry_ref, acc_ref, mask, r0=0):
    n = uo.shape[0]
    ls, lf = _sb_logs(_dot_nt(qs, kb) + _lane_tile(bias, n))
    if mask is not None:
        lf = jnp.where(mask, lf, 0.0)
    lfb = lf.astype(BF16)
    cr = _dot(lfb, uo)
    acc_ref[r0:, :] += _dot(_sb_weights(ls, cr, carry_ref[r0:, :], mask), vb)
    if uo.shape[1] == n:
        tot = cr[:, 0:1] + lfb[:, 0:1].astype(F32)
        carry_ref[r0:, :] += jnp.broadcast_to(tot, (tot.shape[0], LANE))
    else:
        carry_ref[r0:, :] += cr[:, n:]


def _sb_blocks_staged(qs, kb, vb, bias, uo, carry_ref, acc_ref, mask, transposed):
    m = qs[0].shape[0]
    n = uo.shape[0]
    logs = [_sb_logs((_dot(q, k) if transposed else _dot_nt(q, k)) + bias) for q, k in zip(qs, kb)]
    lfs = [(lf if mask is None else jnp.where(mask, lf, 0.0)).astype(BF16) for _, lf in logs]
    cr_all = _dot(jnp.concatenate(lfs, axis=0), uo)
    for g in range(len(qs)):
        cr = cr_all[g * m:(g + 1) * m]
        w = _sb_weights(logs[g][0], cr, carry_ref[g], mask)
        acc_ref[g] += _dot_nt(w, vb[g]) if transposed else _dot(w, vb[g])
        carry_ref[g] += cr[:, n:]


def _attn_prompt_kernel(q_ref, k_ref, v_ref, gate_ref, bias_ref, uo_ref, o_ref, qs_ref, acc_ref, carry_ref, *, nsub):
    qi = pl.program_id(2)
    qb = nsub * BLK
    lane = lax.broadcasted_iota(jnp.int32, (BLK, LANE), 1)
    for g in range(nsub):
        q = q_ref[g * BLK:(g + 1) * BLK, :]
        zero = jnp.zeros_like(q)
        qs_ref[2 * g * BLK:(2 * g + 1) * BLK, :] = jnp.where(lane < DH, q, zero)
        qs_ref[(2 * g + 1) * BLK:(2 * g + 2) * BLK, :] = jnp.where(lane >= DH, q, zero)
    acc_ref[...] = jnp.zeros_like(acc_ref)
    carry_ref[...] = jnp.zeros_like(carry_ref)
    uo = uo_ref[...]
    per = KBLK // BLK
    for c in reversed(range(qb // KBLK)):
        r0 = 2 * c * KBLK
        m = 2 * qb - r0
        row = lax.broadcasted_iota(jnp.int32, (m, KBLK), 0)
        col = lax.broadcasted_iota(jnp.int32, (m, KBLK), 1)
        first_hidden = (row // (2 * BLK)) * BLK + (row & (BLK - 1))
        mask = col < jnp.where(row >= 2 * KBLK, KBLK, first_hidden)
        ks = pl.multiple_of(qi * qb + c * KBLK, KBLK)
        _sb_block(qs_ref[r0:, :], k_ref[pl.ds(ks, KBLK), :], v_ref[pl.ds(ks, KBLK), :], bias_ref[r0:, :], uo,
                  carry_ref, acc_ref, mask, r0=r0)

    def body(t, carry):
        for c in reversed(range(qb // KBLK)):
            ks = pl.multiple_of((qi - 1 - t) * qb + c * KBLK, KBLK)
            _sb_block(qs_ref[...], k_ref[pl.ds(ks, KBLK), :], v_ref[pl.ds(ks, KBLK), :], bias_ref[...], uo,
                      carry_ref, acc_ref, None)
        return carry

    lax.fori_loop(0, qi, body, 0)
    for g in range(nsub):
        o = jnp.where(lane < DH, acc_ref[2 * g * BLK:(2 * g + 1) * BLK, :],
                      acc_ref[(2 * g + 1) * BLK:(2 * g + 2) * BLK, :])
        o_ref[g * BLK:(g + 1) * BLK, :] = (o * _silu(gate_ref[g * BLK:(g + 1) * BLK, :])).astype(o_ref.dtype)


def _attn_prompt(qkv, u, bias_pairs, uo, nseq, tpad, nsub):
    n = nseq * tpad
    qb = nsub * BLK
    nq = tpad // qb
    hp = NH // 2
    kcol, vcol, gcol = OFF_K // LANE, OFF_V // LANE, OFF_SBG // LANE
    return pl.pallas_call(
        functools.partial(_attn_prompt_kernel, nsub=nsub),
        grid=(nseq, hp, nq),
        in_specs=[
            pl.BlockSpec((qb, LANE), lambda b, h, i: (b * nq + i, h)),
            pl.BlockSpec((tpad, LANE), lambda b, h, i: (b, kcol + h)),
            pl.BlockSpec((tpad, LANE), lambda b, h, i: (b, vcol + h)),
            pl.BlockSpec((qb, LANE), lambda b, h, i: (b * nq + i, gcol + h)),
            pl.BlockSpec((None, 2 * qb, LANE), lambda b, h, i: (h, 0, 0)),
            pl.BlockSpec((KBLK, KBLK), lambda b, h, i: (0, 0)),
        ],
        out_specs=pl.BlockSpec((qb, LANE), lambda b, h, i: (b * nq + i, h)),
        out_shape=jax.ShapeDtypeStruct((n, SBW), BF16),
        scratch_shapes=[pltpu.VMEM((2 * qb, LANE), BF16), pltpu.VMEM((2 * qb, LANE), F32),
                        pltpu.VMEM((2 * qb, LANE), F32)],
        compiler_params=_params(("arbitrary", "arbitrary", "arbitrary")),
        name="attn_prompt",
    )(qkv, qkv, qkv, u, bias_pairs, uo)


def _attn_sample_kernel(pt_ref, q_ref, kn_ref, vn_ref, gate_ref, *rest, n_pages, group):
    del pt_ref
    kp_refs, vp_refs = rest[:group], rest[group:2 * group]
    bias_ref, uo_ref, o_ref, qs_ref, acc_ref, carry_ref = rest[2 * group:]
    j = pl.program_id(1)
    s = q_ref.shape[0] // group
    m = NH * s
    page = uo_ref.shape[0]

    @pl.when(j == 0)
    def _():
        head = lax.broadcasted_iota(jnp.int32, (s, SBW), 1) // DH
        row = lax.rem(lax.broadcasted_iota(jnp.int32, (m, page), 0), s)
        col = lax.broadcasted_iota(jnp.int32, (m, page), 1)
        pad = jnp.zeros((page - s, SBW), F32)
        acc_ref[...] = jnp.zeros_like(acc_ref)
        carry_ref[...] = jnp.zeros_like(carry_ref)
        qs, kb, vb = [], [], []
        for g in range(group):
            q = q_ref[g * s:(g + 1) * s, :] * Q_SCALE
            qs.append(jnp.concatenate([jnp.where(head == h, q, 0.0) for h in range(NH)], axis=0).astype(BF16))
            qs_ref[g] = qs[g]
            kb.append(jnp.concatenate([kn_ref[g * s:(g + 1) * s, :], pad], axis=0).astype(BF16))
            vb.append(jnp.concatenate([vn_ref[g * s:(g + 1) * s, :], pad], axis=0).astype(BF16))
        _sb_blocks_staged(qs, kb, vb, bias_ref[...], uo_ref[...], carry_ref, acc_ref, col < row, False)

    @pl.when(j > 0)
    def _():
        _sb_blocks_staged([qs_ref[g] for g in range(group)],
                          [kp_refs[g][...].astype(BF16) for g in range(group)],
                          [vp_refs[g][...].astype(BF16) for g in range(group)],
                          bias_ref[...], uo_ref[...], carry_ref, acc_ref, None, True)

    @pl.when(j == n_pages)
    def _():
        head = lax.broadcasted_iota(jnp.int32, (s, SBW), 1) // DH
        for g in range(group):
            o = jnp.zeros((s, SBW), F32)
            for h in range(NH):
                o = jnp.where(head == h, acc_ref[g, h * s:(h + 1) * s, :], o)
            o_ref[g * s:(g + 1) * s, :] = o * _silu(gate_ref[g * s:(g + 1) * s, :])


def _attn_sample(u, cache_k, cache_v, page_table, bias_rows, uo, layer, nseq, s, group):
    n_pages = page_table.shape[1]
    page = cache_k.shape[3]
    rows = group * s

    def page_map(g):
        return lambda b, j, pt: (layer, pt[b * group + g, n_pages - jnp.maximum(j, 1)], 0, 0)

    page_specs = [pl.BlockSpec((None, None, SBW, page), page_map(g)) for g in range(group)]
    grid_spec = pltpu.PrefetchScalarGridSpec(
        num_scalar_prefetch=1,
        grid=(nseq // group, n_pages + 1),
        in_specs=[
            pl.BlockSpec((rows, SBW), lambda b, j, pt: (b, OFF_Q // SBW)),
            pl.BlockSpec((rows, SBW), lambda b, j, pt: (b, OFF_K // SBW)),
            pl.BlockSpec((rows, SBW), lambda b, j, pt: (b, OFF_V // SBW)),
            pl.BlockSpec((rows, SBW), lambda b, j, pt: (b, OFF_SBG // SBW)),
            *page_specs, *page_specs,
            pl.BlockSpec((NH * s, page), lambda b, j, pt: (0, 0)),
            pl.BlockSpec((page, 2 * page), lambda b, j, pt: (0, 0)),
        ],
        out_specs=pl.BlockSpec((rows, SBW), lambda b, j, pt: (b, 0)),
        scratch_shapes=[pltpu.VMEM((group, NH * s, SBW), BF16), pltpu.VMEM((group, NH * s, SBW), F32),
                        pltpu.VMEM((group, NH * s, page), F32)],
    )
    return pl.pallas_call(
        functools.partial(_attn_sample_kernel, n_pages=n_pages, group=group),
        grid_spec=grid_spec,
        out_shape=jax.ShapeDtypeStruct((nseq * s, SBW), F32),
        compiler_params=_params(("arbitrary", "arbitrary")),
        name="attn_sample",
    )(page_table, u, u, u, u, *([cache_k] * group), *([cache_v] * group), bias_rows, uo)


def _conv_kernel(cin_ref, cg_ref, st_ref, w_ref, b_ref, lng_ref, lnb_ref, pw_ref, o_ref, stout_ref,
                 full_ref, y_ref, *, tb, sub, last_t, state_off):
    t = pl.program_id(1)

    @pl.when(t == 0)
    def _():
        full_ref[0:CPAD, :] = st_ref[...]

    cin = cin_ref[...]
    full_ref[CPAD:CPAD + tb, :] = cin[:, :CCH] * jax.nn.sigmoid(cin[:, CCH:])
    lead = CPAD - (CW - 1)
    for r0 in range(0, tb, sub):
        acc = jnp.zeros((sub, CCH), F32) + b_ref[...]
        for j in range(CW):
            acc = acc + w_ref[j:j + 1, :] * full_ref[r0 + lead + j:r0 + lead + j + sub, :]
        y_ref[r0:r0 + sub, :] = acc
    c = y_ref[...]
    mu = jnp.mean(c, axis=-1, keepdims=True)
    d = c - mu
    var = jnp.mean(d * d, axis=-1, keepdims=True)
    c = d * lax.rsqrt(var + EPS) * lng_ref[...] + lnb_ref[...]
    c = _dot(_silu(c).astype(BF16), pw_ref[...])
    o_ref[...] = (c * _silu(cg_ref[...])).astype(o_ref.dtype)

    @pl.when(t == last_t)
    def _():
        stout_ref[...] = full_ref[state_off:state_off + CPAD, :]

    full_ref[0:CPAD, :] = full_ref[tb:tb + CPAD, :]


def _conv(u, state_pad, dw_w, dw_b, ln_g, ln_b, pw2_bf, layer, nseq, tpad, t_real, tb, out_dtype):
    nt = tpad // tb
    last_t = (t_real - 1) // tb
    state_off = t_real - last_t * tb
    sub = min(tb, 32)
    kern = functools.partial(_conv_kernel, tb=tb, sub=sub, last_t=last_t, state_off=state_off)
    vec = pl.BlockSpec((None, 1, CCH), lambda b, t: (layer, 0, 0))
    return pl.pallas_call(
        kern,
        grid=(nseq, nt),
        in_specs=[
            pl.BlockSpec((tb, 2 * CCH), lambda b, t: (b * nt + t, OFF_CIN // (2 * CCH))),
            pl.BlockSpec((tb, CCH), lambda b, t: (b * nt + t, OFF_CG // CCH)),
            pl.BlockSpec((None, CPAD, CCH), lambda b, t: (b, 0, 0)),
            pl.BlockSpec((None, CPAD, CCH), lambda b, t: (layer, 0, 0)),
            vec, vec, vec,
            pl.BlockSpec((None, CCH, CCH), lambda b, t: (layer, 0, 0)),
        ],
        out_specs=[
            pl.BlockSpec((tb, CCH), lambda b, t: (b * nt + t, 0)),
            pl.BlockSpec((None, CPAD, CCH), lambda b, t: (b, 0, 0)),
        ],
        out_shape=[
            jax.ShapeDtypeStruct((nseq * tpad, CCH), out_dtype),
            jax.ShapeDtypeStruct((nseq, CPAD, CCH), F32),
        ],
        scratch_shapes=[pltpu.VMEM((CPAD + tb, CCH), F32), pltpu.VMEM((tb, CCH), F32)],
        compiler_params=_params(("arbitrary", "arbitrary")),
        name="conv",
    )(u, u, state_pad, dw_w, dw_b, ln_g, ln_b, pw2_bf)


def _gla_kernel(gq_ref, gk_ref, gv_ref, gg_ref, glr_ref, s0_ref, w2h_ref, w2l_ref, gkb_ref, ng_ref,
                ts_ref, gmat_ref, o_ref, sout_ref, s_ref, *, rows, t_real, n_chunks):
    c = pl.program_id(1)

    @pl.when(c == 0)
    def _():
        s_ref[...] = s0_ref[...]

    def padrows(x):
        if rows == BLK:
            return x
        return jnp.concatenate([x, jnp.zeros((BLK - rows, x.shape[1]), x.dtype)], axis=0)

    ridx = lax.broadcasted_iota(jnp.int32, (BLK, GKW), 0)
    valid = ridx < jnp.minimum(rows, t_real - c * rows)
    gq = padrows(gq_ref[...]) * (GDK ** -0.5)
    gk = jnp.where(valid, padrows(gk_ref[...]), 0.0)
    gv = padrows(gv_ref[...])
    glr_h, glr_l = _split2(padrows(glr_ref[...]))
    x = _dot(glr_h, w2h_ref[...]) + _dot(glr_l, w2h_ref[...]) + _dot(glr_h, w2l_ref[...]) + gkb_ref[...]
    lg = jnp.where(valid, _log_sigmoid(x) * (1.0 / GATE_NORM), 0.0)
    ts = ts_ref[...]
    lg_h, lg_m, lg_l = _split3(lg)
    br = _dot(ts, lg_h) + _dot(ts, lg_m) + _dot(ts, lg_l)
    b = br[:BLK]
    r = br[BLK:]
    s_bd = s_ref[...]
    o_inter = _dot((gq * jnp.exp(b)).astype(BF16), s_bd.astype(BF16))
    qd = gq * jnp.exp(b - r)
    gv_bf = gv.astype(BF16)
    nst = GH * GSUB
    qmask = (lax.broadcasted_iota(jnp.int32, (nst, GKW), 1) // GDK) == (lax.broadcasted_iota(jnp.int32, (nst, GKW), 0) // GSUB)
    vmask = (lax.broadcasted_iota(jnp.int32, (nst, GVW), 1) // GDV) == (lax.broadcasted_iota(jnp.int32, (nst, GVW), 0) // GSUB)
    trow = lax.rem(lax.broadcasted_iota(jnp.int32, (nst, BLK), 0), GSUB)
    scol = lax.broadcasted_iota(jnp.int32, (nst, BLK), 1)
    outs = []
    for i in range(BLK // GSUB):
        lo = i * GSUB
        kd = (gk * jnp.exp(r[lo:lo + 1, :] - b)).astype(BF16)
        qd_i = qd[lo:lo + GSUB]
        qst = jnp.where(qmask, jnp.concatenate([qd_i] * GH, axis=0), 0.0).astype(BF16)
        att = jnp.where(scol <= trow + lo, _dot_nt(qst, kd), 0.0)
        ov = jnp.where(vmask, _dot(att.astype(BF16), gv_bf), 0.0)
        o_i = ov[0:GSUB]
        for h in range(1, GH):
            o_i = o_i + ov[h * GSUB:(h + 1) * GSUB]
        outs.append(o_i)
    o = jnp.concatenate(outs, axis=0) + o_inter
    o2_h, o2_l = _split2(o * o)
    ms = _dot(o2_h, gmat_ref[...]) + _dot(o2_l, gmat_ref[...])
    res = o * lax.rsqrt(ms + EPS) * ng_ref[...] * _silu(padrows(gg_ref[...]))
    o_ref[...] = res[:rows].astype(o_ref.dtype)
    kdec = gk * jnp.exp(b[BLK - 1:BLK, :] - b)
    kv = _dot(kdec.T.astype(BF16), gv_bf)
    bl_col = jnp.sum(lg.T, axis=1, keepdims=True)
    bdmask = (lax.broadcasted_iota(jnp.int32, (GKW, GVW), 0) // GDK) == (lax.broadcasted_iota(jnp.int32, (GKW, GVW), 1) // GDV)
    s_new = jnp.exp(bl_col) * s_bd + jnp.where(bdmask, kv, 0.0)
    s_ref[...] = s_new

    @pl.when(c == n_chunks - 1)
    def _():
        sout_ref[...] = s_new


def _gla(u, s0_bd, w2h, w2l, gkb, ng_tiled, ts, gmat, layer, nseq, tpad, t_real, rows, out_dtype):
    nc = tpad // rows
    kern = functools.partial(_gla_kernel, rows=rows, t_real=t_real, n_chunks=nc)
    return pl.pallas_call(
        kern,
        grid=(nseq, nc),
        in_specs=[
            pl.BlockSpec((rows, GKW), lambda b, c: (b * nc + c, OFF_GQ // GKW)),
            pl.BlockSpec((rows, GKW), lambda b, c: (b * nc + c, OFF_GK // GKW)),
            pl.BlockSpec((rows, GVW), lambda b, c: (b * nc + c, OFF_GV // GVW)),
            pl.BlockSpec((rows, GVW), lambda b, c: (b * nc + c, OFF_GG // GVW)),
            pl.BlockSpec((rows, LANE), lambda b, c: (b * nc + c, OFF_GLR // LANE)),
            pl.BlockSpec((None, GKW, GVW), lambda b, c: (b, 0, 0)),
            pl.BlockSpec((None, LANE, GKW), lambda b, c: (layer, 0, 0)),
            pl.BlockSpec((None, LANE, GKW), lambda b, c: (layer, 0, 0)),
            pl.BlockSpec((None, 1, GKW), lambda b, c: (layer, 0, 0)),
            pl.BlockSpec((None, 1, GVW), lambda b, c: (layer, 0, 0)),
            pl.BlockSpec((2 * BLK, BLK), lambda b, c: (0, 0)),
            pl.BlockSpec((GVW, GVW), lambda b, c: (0, 0)),
        ],
        out_specs=[
            pl.BlockSpec((rows, GVW), lambda b, c: (b * nc + c, 0)),
            pl.BlockSpec((None, GKW, GVW), lambda b, c: (b, 0, 0)),
        ],
        out_shape=[
            jax.ShapeDtypeStruct((nseq * tpad, GVW), out_dtype),
            jax.ShapeDtypeStruct((nseq, GKW, GVW), F32),
        ],
        scratch_shapes=[pltpu.VMEM((GKW, GVW), F32)],
        compiler_params=_params(("arbitrary", "arbitrary")),
        name="gla",
    )(u, u, u, u, u, s0_bd, w2h, w2l, gkb, ng_tiled, ts, gmat)


def _state_to_blockdiag(s):
    n = s.shape[0]
    out = jnp.zeros((n, GH, GDK, GH, GDV), F32)
    for h in range(GH):
        out = out.at[:, h, :, h, :].set(s[:, h].astype(F32))
    return out.reshape(n, GKW, GVW)


def _blockdiag_to_state(s_bd):
    n = s_bd.shape[0]
    s5 = s_bd.reshape(n, GH, GDK, GH, GDV)
    return jnp.stack([s5[:, h, :, h, :] for h in range(GH)], axis=1)


def _row_block(n):
    for bm in (256, 128, 64, 32, 16, 8):
        if n % bm == 0:
            return bm
    raise ValueError(f"row count {n} is not a multiple of 8")


def kernel(x_prompt, x_sample, cache_k, cache_v, state_conv, state_gla, page_table, meta_tokens, norm_g, w_in,
           sb_bias, conv_dw_w, conv_dw_b, conv_ln_g, conv_ln_b, conv_pw2, gla_gk_w2, gla_gk_b, gla_norm_g, w_out,
           final_norm_g):
    bp, seq, _ = x_prompt.shape
    bd, dseq, _ = x_sample.shape
    depth = w_in.shape[0]
    n_pool, page = cache_k.shape[1], cache_k.shape[2]
    assert page == BLK and dseq % 8 == 0 and BLK % dseq == 0
    t_real = N_META + seq
    tpad = -(-t_real // KBLK) * KBLK

    w_in_bf = jnp.pad(w_in, ((0, 0), (0, 0), (0, IN_WP - IN_W))).astype(BF16)
    w_out_bf = w_out.astype(BF16)
    norm_g3 = norm_g.reshape(depth, 1, D_MODEL)
    dw_w = jnp.pad(conv_dw_w, ((0, 0), (0, CPAD - CW), (0, 0)))
    dw_b = conv_dw_b.reshape(depth, 1, CCH)
    ln_g = conv_ln_g.reshape(depth, 1, CCH)
    ln_b = conv_ln_b.reshape(depth, 1, CCH)
    pw2_bf = conv_pw2.astype(BF16)
    w2 = jnp.pad(gla_gk_w2, ((0, 0), (0, LANE - GRANK), (0, 0)))
    w2h = w2.astype(BF16)
    w2l = (w2 - w2h.astype(F32)).astype(BF16)
    gkb = gla_gk_b.reshape(depth, 1, GKW)
    ng_tiled = jnp.tile(gla_norm_g, (1, GH)).reshape(depth, 1, GVW)
    ii = jnp.arange(BLK)
    uo = jnp.concatenate([(ii[:, None] > ii[None, :]), jnp.ones((BLK, LANE), bool)], axis=1).astype(BF16)
    ik = jnp.arange(KBLK)
    uo_p = (ik[:, None] > ik[None, :]).astype(BF16)
    tri = ii[None, :] <= ii[:, None]
    sel = ii[None, :] < (ii[:, None] // GSUB) * GSUB
    ts = jnp.concatenate([tri, sel], axis=0).astype(BF16)
    gi = jnp.arange(GVW) // GDV
    gmat = ((gi[:, None] == gi[None, :]).astype(F32) / GDV).astype(BF16)
    final_g = final_norm_g.reshape(1, D_MODEL)

    meta = jnp.broadcast_to(meta_tokens[None].astype(F32), (bp, N_META, D_MODEL))
    xp = jnp.concatenate([meta, x_prompt, jnp.zeros((bp, tpad - t_real, D_MODEL), F32)], axis=1)
    xp = xp.reshape(bp * tpad, D_MODEL)
    xs = x_sample.reshape(bd * dseq, D_MODEL)
    bm_p = _row_block(bp * tpad)
    bm_s = _row_block(bd * dseq)
    conv_tb = BLK
    nsub = max(d for d in (2, 4, 6) if (tpad // BLK) % d == 0)
    group = max(d for d in range(1, 9) if bd % d == 0)
    cache_k4 = jnp.transpose(cache_k, (0, 1, 3, 4, 2)).reshape(depth, n_pool, SBW, page)
    cache_v4 = jnp.transpose(cache_v, (0, 1, 3, 4, 2)).reshape(depth, n_pool, SBW, page)
    zero_conv = jnp.zeros((bp, CPAD, CCH), F32)
    zero_gla = jnp.zeros((bp, GKW, GVW), F32)

    kp_l, vp_l, cp_l, sp_l, ks_l, vs_l, cs_l, ss_l = ([] for _ in range(8))
    for l in range(depth):
        bias2 = sb_bias[l] * LOG2E
        bias_pairs = jnp.broadcast_to(
            jnp.tile(jnp.repeat(bias2.reshape(NH // 2, 2), BLK, axis=1), (1, nsub))[:, :, None],
            (NH // 2, 2 * nsub * BLK, LANE))
        bias_rows = jnp.broadcast_to(jnp.repeat(bias2, dseq)[:, None], (NH * dseq, page))
        last = l == depth - 1
        u_p, qkv_p = _inproj(xp, norm_g3, w_in_bf, l, bm_p)
        mix_sb = _attn_prompt(qkv_p, u_p, bias_pairs, uo_p, bp, tpad, nsub)
        mix_cv, cst_p = _conv(u_p, zero_conv, dw_w, dw_b, ln_g, ln_b, pw2_bf, l, bp, tpad, t_real, conv_tb, BF16)
        mix_gla, gst_p = _gla(u_p, zero_gla, w2h, w2l, gkb, ng_tiled, ts, gmat, l, bp, tpad, t_real, BLK, BF16)
        xp = _outproj(xp, mix_sb, mix_cv, mix_gla, w_out_bf, l, bm_p, final_g if last else None)
        kv = u_p[:, OFF_K:OFF_SBG].reshape(bp, tpad, 2, NH, DH)[:, :t_real]
        kp_l.append(kv[:, :, 0])
        vp_l.append(kv[:, :, 1])
        cp_l.append(cst_p[:, CPAD - (CW - 1):])
        sp_l.append(_blockdiag_to_state(gst_p))
        u_s, _ = _inproj(xs, norm_g3, w_in_bf, l, bm_s)
        smix_sb = _attn_sample(u_s, cache_k4, cache_v4, page_table, bias_rows, uo, l, bd, dseq, group)
        st_s = jnp.pad(state_conv[l], ((0, 0), (CPAD - (CW - 1), 0), (0, 0)))
        smix_cv, cst_s = _conv(u_s, st_s, dw_w, dw_b, ln_g, ln_b, pw2_bf, l, bd, dseq, dseq, dseq, F32)
        smix_gla, gst_s = _gla(u_s, _state_to_blockdiag(state_gla[l]), w2h, w2l, gkb, ng_tiled, ts, gmat, l, bd,
                               dseq, dseq, dseq, F32)
        xs = _outproj(xs, smix_sb, smix_cv, smix_gla, w_out_bf, l, bm_s, final_g if last else None)
        ks_l.append(u_s[:, OFF_K:OFF_V].reshape(bd, dseq, NH, DH))
        vs_l.append(u_s[:, OFF_V:OFF_SBG].reshape(bd, dseq, NH, DH))
        cs_l.append(cst_s[:, CPAD - (CW - 1):])
        ss_l.append(_blockdiag_to_state(gst_s))

    y_prompt = xp.reshape(bp, tpad, D_MODEL)[:, N_META:t_real]
    y_sample = xs.reshape(bd, dseq, D_MODEL)
    return (y_prompt, y_sample, jnp.stack(kp_l), jnp.stack(vp_l), jnp.stack(cp_l), jnp.stack(sp_l),
            jnp.stack(ks_l), jnp.stack(vs_l), jnp.stack(cs_l), jnp.stack(ss_l))
```

```python
import functools

import jax
import jax.numpy as jnp
from jax import lax
from jax.experimental import pallas as pl
from jax.experimental.pallas import tpu as pltpu

F32 = jnp.float32
BF16 = jnp.bfloat16

D_MODEL = 1024
N_META = 16
NH = 8
DH = 64
SBW = NH * DH
CCH = 256
CW = 31
GH = 4
GDK = 32
GDV = 64
GKW = GH * GDK
GVW = GH * GDV
GRANK = 16
GATE_NORM = 16.0
EPS = 1e-6

LANE = 128
MXU_DIM = 256
BLK = 128
KBLK = MXU_DIM
LOG2E = 1.4426950408889634
Q_SCALE = DH ** -0.5 * LOG2E
N_BIAS_PIECES = 3
GSUB = 32
CPAD = 32
IN_W = 3600
IN_WP = 3712
OFF_Q, OFF_K, OFF_V, OFF_SBG = 0, 512, 1024, 1536
OFF_CIN, OFF_CG = 2048, 2560
OFF_GQ, OFF_GK, OFF_GV, OFF_GG, OFF_GLR = 2816, 2944, 3072, 3328, 3584
VMEM_LIMIT = 48 * 1024 * 1024


def _dot(a, b):
    return jnp.dot(a, b, preferred_element_type=F32)


def _dot_nt(a, b):
    return lax.dot_general(a, b, (((1,), (1,)), ((), ())), preferred_element_type=F32)


def _split2(x):
    hi = x.astype(BF16)
    lo = (x - hi.astype(F32)).astype(BF16)
    return hi, lo


def _split3(x):
    hi = x.astype(BF16)
    r = x - hi.astype(F32)
    mid = r.astype(BF16)
    lo = (r - mid.astype(F32)).astype(BF16)
    return hi, mid, lo


def _log_sigmoid(x):
    return jnp.minimum(x, 0.0) - jnp.log(1.0 + jnp.exp(-jnp.abs(x)))


def _silu(x):
    return x * jax.nn.sigmoid(x)


def _params(sem):
    return pltpu.CompilerParams(dimension_semantics=sem, vmem_limit_bytes=VMEM_LIMIT)


def _inproj_kernel(x_ref, g_ref, w_ref, u_ref):
    x = x_ref[...]
    ms = jnp.mean(x * x, axis=-1, keepdims=True)
    h = (x * lax.rsqrt(ms + EPS) * g_ref[...]).astype(BF16)
    for c0 in range(0, IN_WP, SBW):
        wd = min(SBW, IN_WP - c0)
        u_ref[:, c0:c0 + wd] = _dot(h, w_ref[:, c0:c0 + wd])


def _inproj_prompt_kernel(x_ref, g_ref, w_ref, wkv_ref, *rest):
    u_ref, q_ref, kt_ref, vt_ref, kvbf_ref = rest[-5:]
    x = x_ref[...]
    ms = jnp.mean(x * x, axis=-1, keepdims=True)
    h = (x * lax.rsqrt(ms + EPS) * g_ref[...]).astype(BF16)
    q_ref[...] = (_dot(h, w_ref[:, OFF_Q:OFF_Q + SBW]) * Q_SCALE).astype(BF16)
    for c0 in range(OFF_SBG, IN_WP, SBW):
        wd = min(SBW, IN_WP - c0)
        u_ref[:, c0 - OFF_SBG:c0 - OFF_SBG + wd] = _dot(h, w_ref[:, c0:c0 + wd])
    kvt = _dot_nt(wkv_ref[...], h)
    kt_ref[...] = kvt[:SBW]
    vt_ref[...] = kvt[SBW:]
    kvbf_ref[...] = kvt.astype(BF16)


def _inproj_prompt(x, norm_g, w_in_bf, wkv_t, kt_prev, vt_prev, layer, nseq, tpad, t_real, depth, bm):
    n = x.shape[0]
    nt = tpad // bm
    in_specs = [
        pl.BlockSpec((bm, D_MODEL), lambda b, t: (b * nt + t, 0)),
        pl.BlockSpec((None, 1, D_MODEL), lambda b, t: (layer, 0, 0)),
        pl.BlockSpec((None, D_MODEL, IN_WP), lambda b, t: (layer, 0, 0)),
        pl.BlockSpec((None, 2 * SBW, D_MODEL), lambda b, t: (layer, 0, 0)),
    ]
    args = [x, norm_g, w_in_bf, wkv_t]
    aliases = {}
    if kt_prev is not None:
        in_specs += [pl.BlockSpec(memory_space=pl.ANY), pl.BlockSpec(memory_space=pl.ANY)]
        args += [kt_prev, vt_prev]
        aliases = {4: 2, 5: 3}
    cache = jax.ShapeDtypeStruct((depth, nseq, SBW, tpad), F32)
    cache_spec = pl.BlockSpec((None, None, SBW, bm), lambda b, t: (layer, b, 0, t))
    return pl.pallas_call(
        _inproj_prompt_kernel,
        grid=(nseq, nt),
        in_specs=in_specs,
        out_specs=[
            pl.BlockSpec((bm, IN_WP - OFF_SBG), lambda b, t: (b * nt + t, 0)),
            pl.BlockSpec((bm, SBW), lambda b, t: (b * nt + t, 0)),
            cache_spec, cache_spec,
            pl.BlockSpec((None, 2 * SBW, bm), lambda b, t: (b, 0, t)),
        ],
        out_shape=[
            jax.ShapeDtypeStruct((n, IN_WP - OFF_SBG), F32),
            jax.ShapeDtypeStruct((n, SBW), BF16),
            cache, cache,
            jax.ShapeDtypeStruct((nseq, 2 * SBW, tpad), BF16),
        ],
        input_output_aliases=aliases,
        compiler_params=_params(("parallel", "parallel")),
        name="inproj_prompt",
    )(*args)


def _inproj(x, norm_g, w_in_bf, layer, bm):
    n = x.shape[0]
    return pl.pallas_call(
        _inproj_kernel,
        grid=(n // bm,),
        in_specs=[
            pl.BlockSpec((bm, D_MODEL), lambda i: (i, 0)),
            pl.BlockSpec((None, 1, D_MODEL), lambda i: (layer, 0, 0)),
            pl.BlockSpec((None, D_MODEL, IN_WP), lambda i: (layer, 0, 0)),
        ],
        out_specs=pl.BlockSpec((bm, IN_WP), lambda i: (i, 0)),
        out_shape=jax.ShapeDtypeStruct((n, IN_WP), F32),
        compiler_params=_params(("parallel",)),
        name="inproj",
    )(x, norm_g, w_in_bf)


def _outproj_kernel(x_ref, a_ref, b_ref, c_ref, w_ref, *rest, final):
    y = x_ref[...]
    y = y + _dot(a_ref[...].astype(BF16), w_ref[0:SBW, :])
    y = y + _dot(b_ref[...].astype(BF16), w_ref[SBW:SBW + CCH, :])
    y = y + _dot(c_ref[...].astype(BF16), w_ref[SBW + CCH:, :])
    if final:
        g_ref, y_ref = rest
        ms = jnp.mean(y * y, axis=-1, keepdims=True)
        y_ref[...] = y * lax.rsqrt(ms + EPS) * g_ref[...]
    else:
        (y_ref,) = rest
        y_ref[...] = y


def _outproj(x, mix_sb, mix_cv, mix_gla, w_out_bf, layer, bm, final_g=None):
    n = x.shape[0]
    final = final_g is not None
    in_specs = [
        pl.BlockSpec((bm, D_MODEL), lambda i: (i, 0)),
        pl.BlockSpec((bm, SBW), lambda i: (i, 0)),
        pl.BlockSpec((bm, CCH), lambda i: (i, 0)),
        pl.BlockSpec((bm, GVW), lambda i: (i, 0)),
        pl.BlockSpec((None, D_MODEL, D_MODEL), lambda i: (layer, 0, 0)),
    ]
    args = [x, mix_sb, mix_cv, mix_gla, w_out_bf]
    if final:
        in_specs.append(pl.BlockSpec((1, D_MODEL), lambda i: (0, 0)))
        args.append(final_g)
    return pl.pallas_call(
        functools.partial(_outproj_kernel, final=final),
        grid=(n // bm,),
        in_specs=in_specs,
        out_specs=pl.BlockSpec((bm, D_MODEL), lambda i: (i, 0)),
        out_shape=jax.ShapeDtypeStruct((n, D_MODEL), F32),
        compiler_params=_params(("parallel",)),
        name="outproj",
    )(*args)


def _neg_abs(x):
    bits = lax.bitcast_convert_type(x, jnp.uint32) | jnp.uint32(0x80000000)
    return lax.bitcast_convert_type(bits, F32)


def _sb_logs(z):
    l = jnp.log(1.0 + jnp.exp2(_neg_abs(z))) * LOG2E
    ls = jnp.minimum(z, 0.0) - l
    return ls, ls - z


def _lane_tile(x, n):
    reps = n // x.shape[1]
    return x if reps == 1 else jnp.concatenate([x] * reps, axis=1)


def _sb_weights(ls, cr, carry, mask):
    n = ls.shape[1]
    w = jnp.exp2(ls + cr[:, :n] + _lane_tile(carry, n))
    if mask is not None:
        w = jnp.where(mask, w, 0.0)
    return w.astype(BF16)


def _sb_block(qs, kb, vb, uo, carry_ref, acc_ref, mask, r0=0):
    ls, lf = _sb_logs(_dot(qs, kb))
    if mask is not None:
        lf = jnp.where(mask, lf, 0.0)
    lfb = lf.astype(BF16)
    cr = _dot(lfb, uo)
    acc_ref[r0:, :] += _dot_nt(_sb_weights(ls, cr, carry_ref[r0:, :], mask), vb)
    tot = cr[:, 0:1] + lfb[:, 0:1].astype(F32)
    carry_ref[r0:, :] += jnp.broadcast_to(tot, (tot.shape[0], LANE))


def _sb_blocks_staged(qs, kb, vb, bias, uo, carry_ref, acc_ref, mask, transposed):
    m = qs[0].shape[0]
    n = uo.shape[0]
    logs = [_sb_logs((_dot(q, k) if transposed else _dot_nt(q, k)) + bias) for q, k in zip(qs, kb)]
    lfs = [(lf if mask is None else jnp.where(mask, lf, 0.0)).astype(BF16) for _, lf in logs]
    cr_all = _dot(jnp.concatenate(lfs, axis=0), uo)
    for g in range(len(qs)):
        cr = cr_all[g * m:(g + 1) * m]
        w = _sb_weights(logs[g][0], cr, carry_ref[g], mask)
        acc_ref[g] += _dot_nt(w, vb[g]) if transposed else _dot(w, vb[g])
        carry_ref[g] += cr[:, n:]


def _attn_prompt_kernel(q_ref, k_ref, v_ref, gate_ref, bias_ref, uo_ref, o_ref, qs_ref, acc_ref, carry_ref, *, nsub):
    qi = pl.program_id(2)
    qb = nsub * BLK
    lane = lax.broadcasted_iota(jnp.int32, (BLK, LANE), 1)
    for g in range(nsub):
        q = q_ref[g * BLK:(g + 1) * BLK, :]
        zero = jnp.zeros_like(q)
        qs_ref[2 * g * BLK:(2 * g + 1) * BLK, :LANE] = jnp.where(lane < DH, q, zero)
        qs_ref[(2 * g + 1) * BLK:(2 * g + 2) * BLK, :LANE] = jnp.where(lane >= DH, q, zero)
    qs_ref[:, LANE:] = bias_ref[...]
    ones_rows = jnp.where(lax.broadcasted_iota(jnp.int32, (LANE, KBLK), 0) < N_BIAS_PIECES, 1.0, 0.0).astype(BF16)
    acc_ref[...] = jnp.zeros_like(acc_ref)
    carry_ref[...] = jnp.zeros_like(carry_ref)
    uo = uo_ref[...]
    per = KBLK // BLK
    for c in reversed(range(qb // KBLK)):
        r0 = 2 * c * KBLK
        m = 2 * qb - r0
        row = lax.broadcasted_iota(jnp.int32, (m, KBLK), 0)
        col = lax.broadcasted_iota(jnp.int32, (m, KBLK), 1)
        first_hidden = (row // (2 * BLK)) * BLK + (row & (BLK - 1))
        mask = col < jnp.where(row >= 2 * KBLK, KBLK, first_hidden)
        ks = pl.multiple_of(qi * qb + c * KBLK, KBLK)
        kb = jnp.concatenate([k_ref[:, pl.ds(ks, KBLK)], ones_rows], axis=0)
        _sb_block(qs_ref[r0:, :], kb, v_ref[:, pl.ds(ks, KBLK)], uo, carry_ref, acc_ref, mask, r0=r0)

    def body(t, carry):
        for c in reversed(range(qb // KBLK)):
            ks = pl.multiple_of((qi - 1 - t) * qb + c * KBLK, KBLK)
            kb = jnp.concatenate([k_ref[:, pl.ds(ks, KBLK)], ones_rows], axis=0)
            _sb_block(qs_ref[...], kb, v_ref[:, pl.ds(ks, KBLK)], uo, carry_ref, acc_ref, None)
        return carry

    lax.fori_loop(0, qi, body, 0)
    for g in range(nsub):
        o = jnp.where(lane < DH, acc_ref[2 * g * BLK:(2 * g + 1) * BLK, :],
                      acc_ref[(2 * g + 1) * BLK:(2 * g + 2) * BLK, :])
        o_ref[g * BLK:(g + 1) * BLK, :] = (o * _silu(gate_ref[g * BLK:(g + 1) * BLK, :])).astype(o_ref.dtype)


def _attn_prompt(q, kvt, u, ubase, bias_pairs, uo, nseq, tpad, nsub):
    n = nseq * tpad
    qb = nsub * BLK
    nq = tpad // qb
    hp = NH // 2
    gcol = (OFF_SBG - ubase) // LANE
    return pl.pallas_call(
        functools.partial(_attn_prompt_kernel, nsub=nsub),
        grid=(nseq, hp, nq),
        in_specs=[
            pl.BlockSpec((qb, LANE), lambda b, h, i: (b * nq + i, h)),
            pl.BlockSpec((None, LANE, tpad), lambda b, h, i: (b, h, 0)),
            pl.BlockSpec((None, LANE, tpad), lambda b, h, i: (b, hp + h, 0)),
            pl.BlockSpec((qb, LANE), lambda b, h, i: (b * nq + i, gcol + h)),
            pl.BlockSpec((None, 2 * qb, LANE), lambda b, h, i: (h, 0, 0)),
            pl.BlockSpec((KBLK, KBLK), lambda b, h, i: (0, 0)),
        ],
        out_specs=pl.BlockSpec((qb, LANE), lambda b, h, i: (b * nq + i, h)),
        out_shape=jax.ShapeDtypeStruct((n, SBW), BF16),
        scratch_shapes=[pltpu.VMEM((2 * qb, 2 * LANE), BF16), pltpu.VMEM((2 * qb, LANE), F32),
                        pltpu.VMEM((2 * qb, LANE), F32)],
        compiler_params=_params(("arbitrary", "arbitrary", "arbitrary")),
        name="attn_prompt",
    )(q, kvt, kvt, u, bias_pairs, uo)


def _attn_sample_kernel(pt_ref, q_ref, kn_ref, vn_ref, gate_ref, *rest, n_pages, group):
    del pt_ref
    kp_refs, vp_refs = rest[:group], rest[group:2 * group]
    bias_ref, uo_ref, o_ref, qs_ref, acc_ref, carry_ref = rest[2 * group:]
    j = pl.program_id(1)
    s = q_ref.shape[0] // group
    m = NH * s
    page = uo_ref.shape[0]

    @pl.when(j == 0)
    def _():
        head = lax.broadcasted_iota(jnp.int32, (s, SBW), 1) // DH
        row = lax.rem(lax.broadcasted_iota(jnp.int32, (m, page), 0), s)
        col = lax.broadcasted_iota(jnp.int32, (m, page), 1)
        pad = jnp.zeros((page - s, SBW), F32)
        acc_ref[...] = jnp.zeros_like(acc_ref)
        carry_ref[...] = jnp.zeros_like(carry_ref)
        qs, kb, vb = [], [], []
        for g in range(group):
            q = q_ref[g * s:(g + 1) * s, :] * Q_SCALE
            qs.append(jnp.concatenate([jnp.where(head == h, q, 0.0) for h in range(NH)], axis=0).astype(BF16))
            qs_ref[g] = qs[g]
            kb.append(jnp.concatenate([kn_ref[g * s:(g + 1) * s, :], pad], axis=0).astype(BF16))
            vb.append(jnp.concatenate([vn_ref[g * s:(g + 1) * s, :], pad], axis=0).astype(BF16))
        _sb_blocks_staged(qs, kb, vb, bias_ref[...], uo_ref[...], carry_ref, acc_ref, col < row, False)

    @pl.when(j > 0)
    def _():
        _sb_blocks_staged([qs_ref[g] for g in range(group)],
                          [kp_refs[g][...].astype(BF16) for g in range(group)],
                          [vp_refs[g][...].astype(BF16) for g in range(group)],
                          bias_ref[...], uo_ref[...], carry_ref, acc_ref, None, True)

    @pl.when(j == n_pages)
    def _():
        head = lax.broadcasted_iota(jnp.int32, (s, SBW), 1) // DH
        for g in range(group):
            o = jnp.zeros((s, SBW), F32)
            for h in range(NH):
                o = jnp.where(head == h, acc_ref[g, h * s:(h + 1) * s, :], o)
            o_ref[g * s:(g + 1) * s, :] = o * _silu(gate_ref[g * s:(g + 1) * s, :])


def _attn_sample(u, cache_k, cache_v, page_table, bias_rows, uo, layer, nseq, s, group):
    n_pages = page_table.shape[1]
    page = cache_k.shape[3]
    rows = group * s

    def page_map(g):
        return lambda b, j, pt: (layer, pt[b * group + g, n_pages - jnp.maximum(j, 1)], 0, 0)

    page_specs = [pl.BlockSpec((None, None, SBW, page), page_map(g)) for g in range(group)]
    grid_spec = pltpu.PrefetchScalarGridSpec(
        num_scalar_prefetch=1,
        grid=(nseq // group, n_pages + 1),
        in_specs=[
            pl.BlockSpec((rows, SBW), lambda b, j, pt: (b, OFF_Q // SBW)),
            pl.BlockSpec((rows, SBW), lambda b, j, pt: (b, OFF_K // SBW)),
            pl.BlockSpec((rows, SBW), lambda b, j, pt: (b, OFF_V // SBW)),
            pl.BlockSpec((rows, SBW), lambda b, j, pt: (b, OFF_SBG // SBW)),
            *page_specs, *page_specs,
            pl.BlockSpec((NH * s, page), lambda b, j, pt: (0, 0)),
            pl.BlockSpec((page, 2 * page), lambda b, j, pt: (0, 0)),
        ],
        out_specs=pl.BlockSpec((rows, SBW), lambda b, j, pt: (b, 0)),
        scratch_shapes=[pltpu.VMEM((group, NH * s, SBW), BF16), pltpu.VMEM((group, NH * s, SBW), F32),
                        pltpu.VMEM((group, NH * s, page), F32)],
    )
    return pl.pallas_call(
        functools.partial(_attn_sample_kernel, n_pages=n_pages, group=group),
        grid_spec=grid_spec,
        out_shape=jax.ShapeDtypeStruct((nseq * s, SBW), F32),
        compiler_params=_params(("arbitrary", "arbitrary")),
        name="attn_sample",
    )(page_table, u, u, u, u, *([cache_k] * group), *([cache_v] * group), bias_rows, uo)


def _conv_kernel(cin_ref, cg_ref, st_ref, w_ref, b_ref, lng_ref, lnb_ref, pw_ref, o_ref, stout_ref,
                 full_ref, shift_ref, y_ref, *, tb, sub, last_t, state_off):
    t = pl.program_id(1)

    @pl.when(t == 0)
    def _():
        full_ref[0:CPAD, :] = st_ref[...]

    cin = cin_ref[...]
    full_ref[CPAD:CPAD + tb, :] = cin[:, :CCH] * jax.nn.sigmoid(cin[:, CCH:])
    nshift = shift_ref.shape[1]
    for s in range(1, 8):
        shift_ref[s - 1] = full_ref[s:s + nshift, :]
    lead = CPAD - (CW - 1)
    for r0 in range(0, tb, sub):
        acc = jnp.zeros((sub, CCH), F32) + b_ref[...]
        for j in range(CW):
            a, s = divmod(lead + j, 8)
            lo = r0 + 8 * a
            rows = full_ref[lo:lo + sub, :] if s == 0 else shift_ref[s - 1, lo:lo + sub, :]
            acc = acc + jnp.concatenate([w_ref[8 * j:8 * j + 8, :]] * (sub // 8), axis=0) * rows
        y_ref[r0:r0 + sub, :] = acc
    c = y_ref[...]
    mu = jnp.mean(c, axis=-1, keepdims=True)
    d = c - mu
    var = jnp.mean(d * d, axis=-1, keepdims=True)
    c = d * lax.rsqrt(var + EPS) * lng_ref[...] + lnb_ref[...]
    c = _dot(_silu(c).astype(BF16), pw_ref[...])
    o_ref[...] = (c * _silu(cg_ref[...])).astype(o_ref.dtype)

    @pl.when(t == last_t)
    def _():
        stout_ref[...] = full_ref[state_off:state_off + CPAD, :]

    full_ref[0:CPAD, :] = full_ref[tb:tb + CPAD, :]


def _conv(u, ubase, state_pad, dw_w, dw_b, ln_g, ln_b, pw2_bf, layer, nseq, tpad, t_real, tb, out_dtype):
    nt = tpad // tb
    last_t = (t_real - 1) // tb
    state_off = t_real - last_t * tb
    sub = min(tb, 32)
    kern = functools.partial(_conv_kernel, tb=tb, sub=sub, last_t=last_t, state_off=state_off)
    vec = pl.BlockSpec((None, 1, CCH), lambda b, t: (layer, 0, 0))
    return pl.pallas_call(
        kern,
        grid=(nseq, nt),
        in_specs=[
            pl.BlockSpec((tb, 2 * CCH), lambda b, t: (b * nt + t, (OFF_CIN - ubase) // (2 * CCH))),
            pl.BlockSpec((tb, CCH), lambda b, t: (b * nt + t, (OFF_CG - ubase) // CCH)),
            pl.BlockSpec((None, CPAD, CCH), lambda b, t: (b, 0, 0)),
            pl.BlockSpec((None, 8 * CW, CCH), lambda b, t: (layer, 0, 0)),
            vec, vec, vec,
            pl.BlockSpec((None, CCH, CCH), lambda b, t: (layer, 0, 0)),
        ],
        out_specs=[
            pl.BlockSpec((tb, CCH), lambda b, t: (b * nt + t, 0)),
            pl.BlockSpec((None, CPAD, CCH), lambda b, t: (b, 0, 0)),
        ],
        out_shape=[
            jax.ShapeDtypeStruct((nseq * tpad, CCH), out_dtype),
            jax.ShapeDtypeStruct((nseq, CPAD, CCH), F32),
        ],
        scratch_shapes=[pltpu.VMEM((CPAD + tb, CCH), F32), pltpu.VMEM((7, tb + CPAD - 8, CCH), F32),
                        pltpu.VMEM((tb, CCH), F32)],
        compiler_params=_params(("arbitrary", "arbitrary")),
        name="conv",
    )(u, u, state_pad, dw_w, dw_b, ln_g, ln_b, pw2_bf)


def _gla_kernel(gq_ref, gk_ref, gv_ref, gg_ref, glr_ref, s0_ref, w2h_ref, w2l_ref, gkb_ref, ng_ref,
                ts_ref, gmat_ref, o_ref, sout_ref, s_ref, *, rows, t_real, n_chunks):
    c = pl.program_id(1)
    seqs = range(gq_ref.shape[0])

    @pl.when(c == 0)
    def _():
        s_ref[...] = s0_ref[...]

    def padrows(x):
        if rows == BLK:
            return x
        return jnp.concatenate([x, jnp.zeros((BLK - rows, x.shape[1]), x.dtype)], axis=0)

    def rows_of(x, g):
        return x[g * BLK:(g + 1) * BLK]

    ridx = lax.broadcasted_iota(jnp.int32, (BLK, GKW), 0)
    valid = ridx < jnp.minimum(rows, t_real - c * rows)
    valid_all = jnp.concatenate([valid] * len(seqs), axis=0)
    gq = [padrows(gq_ref[g]) * (GDK ** -0.5) for g in seqs]
    gk = [jnp.where(valid, padrows(gk_ref[g]), 0.0) for g in seqs]
    gv_bf = [padrows(gv_ref[g]).astype(BF16) for g in seqs]
    glr_h, glr_l = _split2(jnp.concatenate([padrows(glr_ref[g]) for g in seqs], axis=0))
    x = _dot(glr_h, w2h_ref[...]) + _dot(glr_l, w2h_ref[...]) + _dot(glr_h, w2l_ref[...]) + gkb_ref[...]
    lg_all = jnp.where(valid_all, _log_sigmoid(x) * (1.0 / GATE_NORM), 0.0)
    lg = [rows_of(lg_all, g) for g in seqs]
    ts = ts_ref[...]
    lg_h, lg_m, lg_l = _split3(jnp.concatenate(lg, axis=1))
    br = _dot(ts, lg_h) + _dot(ts, lg_m) + _dot(ts, lg_l)
    b = [br[:BLK, g * GKW:(g + 1) * GKW] for g in seqs]
    r = [br[BLK:, g * GKW:(g + 1) * GKW] for g in seqs]
    s_bd = [s_ref[g] for g in seqs]
    o_inter = [_dot((gq[g] * jnp.exp(b[g])).astype(BF16), s_bd[g].astype(BF16)) for g in seqs]
    qd = [gq[g] * jnp.exp(b[g] - r[g]) for g in seqs]
    nst = GH * GSUB
    qmask = (lax.broadcasted_iota(jnp.int32, (nst, GKW), 1) // GDK) == (lax.broadcasted_iota(jnp.int32, (nst, GKW), 0) // GSUB)
    vmask = (lax.broadcasted_iota(jnp.int32, (nst, GVW), 1) // GDV) == (lax.broadcasted_iota(jnp.int32, (nst, GVW), 0) // GSUB)
    trow = lax.rem(lax.broadcasted_iota(jnp.int32, (nst, BLK), 0), GSUB)
    scol = lax.broadcasted_iota(jnp.int32, (nst, BLK), 1)
    outs = [[] for _ in seqs]
    for i in range(BLK // GSUB):
        lo = i * GSUB
        kd = [(gk[g] * jnp.exp(r[g][lo:lo + 1, :] - b[g])).astype(BF16) for g in seqs]
        qst = [jnp.where(qmask, jnp.concatenate([qd[g][lo:lo + GSUB]] * GH, axis=0), 0.0).astype(BF16) for g in seqs]
        att = [jnp.where(scol <= trow + lo, _dot_nt(qst[g], kd[g]), 0.0).astype(BF16) for g in seqs]
        ov = [jnp.where(vmask, _dot(att[g], gv_bf[g]), 0.0) for g in seqs]
        for g in seqs:
            o_i = ov[g][0:GSUB]
            for h in range(1, GH):
                o_i = o_i + ov[g][h * GSUB:(h + 1) * GSUB]
            outs[g].append(o_i)
    o = jnp.concatenate([jnp.concatenate(outs[g], axis=0) + o_inter[g] for g in seqs], axis=0)
    o2_h, o2_l = _split2(o * o)
    ms = _dot(o2_h, gmat_ref[...]) + _dot(o2_l, gmat_ref[...])
    gate = jnp.concatenate([padrows(gg_ref[g]) for g in seqs], axis=0)
    res = o * lax.rsqrt(ms + EPS) * ng_ref[...] * _silu(gate)
    for g in seqs:
        o_ref[g] = rows_of(res, g)[:rows].astype(o_ref.dtype)
    kdec_t = [(gk[g] * jnp.exp(b[g][BLK - 1:BLK, :] - b[g])).T.astype(BF16) for g in seqs]
    kv = [_dot(kdec_t[g], gv_bf[g]) for g in seqs]
    bdmask = (lax.broadcasted_iota(jnp.int32, (GKW, GVW), 0) // GDK) == (lax.broadcasted_iota(jnp.int32, (GKW, GVW), 1) // GDV)
    for g in seqs:
        bl_col = jnp.sum(lg[g].T, axis=1, keepdims=True)
        s_ref[g] = jnp.exp(bl_col) * s_bd[g] + jnp.where(bdmask, kv[g], 0.0)

    @pl.when(c == n_chunks - 1)
    def _():
        sout_ref[...] = s_ref[...]


def _gla(u, ubase, s0_bd, w2h, w2l, gkb, ng_tiled, ts, gmat, layer, nseq, tpad, t_real, rows, group, out_dtype):
    nc = tpad // rows
    u3 = u.reshape(nseq, tpad, u.shape[1])
    kern = functools.partial(_gla_kernel, rows=rows, t_real=t_real, n_chunks=nc)

    def cols(off, width):
        return pl.BlockSpec((group, rows, width), lambda b, c: (b, c, (off - ubase) // width))

    state_spec = pl.BlockSpec((group, GKW, GVW), lambda b, c: (b, 0, 0))
    o, s_out = pl.pallas_call(
        kern,
        grid=(nseq // group, nc),
        in_specs=[
            cols(OFF_GQ, GKW), cols(OFF_GK, GKW), cols(OFF_GV, GVW), cols(OFF_GG, GVW), cols(OFF_GLR, LANE),
            state_spec,
            pl.BlockSpec((None, LANE, GKW), lambda b, c: (layer, 0, 0)),
            pl.BlockSpec((None, LANE, GKW), lambda b, c: (layer, 0, 0)),
            pl.BlockSpec((None, 1, GKW), lambda b, c: (layer, 0, 0)),
            pl.BlockSpec((None, 1, GVW), lambda b, c: (layer, 0, 0)),
            pl.BlockSpec((2 * BLK, BLK), lambda b, c: (0, 0)),
            pl.BlockSpec((GVW, GVW), lambda b, c: (0, 0)),
        ],
        out_specs=[
            pl.BlockSpec((group, rows, GVW), lambda b, c: (b, c, 0)),
            state_spec,
        ],
        out_shape=[
            jax.ShapeDtypeStruct((nseq, tpad, GVW), out_dtype),
            jax.ShapeDtypeStruct((nseq, GKW, GVW), F32),
        ],
        scratch_shapes=[pltpu.VMEM((group, GKW, GVW), F32)],
        compiler_params=_params(("arbitrary", "arbitrary")),
        name="gla",
    )(u3, u3, u3, u3, u3, s0_bd, w2h, w2l, gkb, ng_tiled, ts, gmat)
    return o.reshape(nseq * tpad, GVW), s_out


def _state_to_blockdiag(s):
    n = s.shape[0]
    out = jnp.zeros((n, GH, GDK, GH, GDV), F32)
    for h in range(GH):
        out = out.at[:, h, :, h, :].set(s[:, h].astype(F32))
    return out.reshape(n, GKW, GVW)


def _blockdiag_to_state(s_bd):
    n = s_bd.shape[0]
    s5 = s_bd.reshape(n, GH, GDK, GH, GDV)
    return jnp.stack([s5[:, h, :, h, :] for h in range(GH)], axis=1)


def _row_block(n):
    for bm in (256, 128, 64, 32, 16, 8):
        if n % bm == 0:
            return bm
    raise ValueError(f"row count {n} is not a multiple of 8")


def kernel(x_prompt, x_sample, cache_k, cache_v, state_conv, state_gla, page_table, meta_tokens, norm_g, w_in,
           sb_bias, conv_dw_w, conv_dw_b, conv_ln_g, conv_ln_b, conv_pw2, gla_gk_w2, gla_gk_b, gla_norm_g, w_out,
           final_norm_g):
    bp, seq, _ = x_prompt.shape
    bd, dseq, _ = x_sample.shape
    depth = w_in.shape[0]
    n_pool, page = cache_k.shape[1], cache_k.shape[2]
    assert page == BLK and dseq % 8 == 0 and BLK % dseq == 0
    t_real = N_META + seq
    tpad = -(-t_real // KBLK) * KBLK

    w_in_bf = jnp.pad(w_in, ((0, 0), (0, 0), (0, IN_WP - IN_W))).astype(BF16)
    wkv_t = jnp.transpose(w_in[:, :, OFF_K:OFF_SBG], (0, 2, 1)).astype(BF16)
    w_out_bf = w_out.astype(BF16)
    norm_g3 = norm_g.reshape(depth, 1, D_MODEL)
    dw_w = jnp.repeat(conv_dw_w, 8, axis=1)
    dw_b = conv_dw_b.reshape(depth, 1, CCH)
    ln_g = conv_ln_g.reshape(depth, 1, CCH)
    ln_b = conv_ln_b.reshape(depth, 1, CCH)
    pw2_bf = conv_pw2.astype(BF16)
    w2 = jnp.pad(gla_gk_w2, ((0, 0), (0, LANE - GRANK), (0, 0)))
    w2h = w2.astype(BF16)
    w2l = (w2 - w2h.astype(F32)).astype(BF16)
    gkb = gla_gk_b.reshape(depth, 1, GKW)
    ng_tiled = jnp.tile(gla_norm_g, (1, GH)).reshape(depth, 1, GVW)
    ii = jnp.arange(BLK)
    uo = jnp.concatenate([(ii[:, None] > ii[None, :]), jnp.ones((BLK, LANE), bool)], axis=1).astype(BF16)
    ik = jnp.arange(KBLK)
    uo_p = (ik[:, None] > ik[None, :]).astype(BF16)
    tri = ii[None, :] <= ii[:, None]
    sel = ii[None, :] < (ii[:, None] // GSUB) * GSUB
    ts = jnp.concatenate([tri, sel], axis=0).astype(BF16)
    gi = jnp.arange(GVW) // GDV
    gmat = ((gi[:, None] == gi[None, :]).astype(F32) / GDV).astype(BF16)
    final_g = final_norm_g.reshape(1, D_MODEL)

    meta = jnp.broadcast_to(meta_tokens[None].astype(F32), (bp, N_META, D_MODEL))
    xp = jnp.concatenate([meta, x_prompt, jnp.zeros((bp, tpad - t_real, D_MODEL), F32)], axis=1)
    xp = xp.reshape(bp * tpad, D_MODEL)
    xs = x_sample.reshape(bd * dseq, D_MODEL)
    bm_p = _row_block(bp * tpad)
    bm_s = _row_block(bd * dseq)
    conv_tb = KBLK if tpad % KBLK == 0 else BLK
    nsub = max(d for d in (2, 4, 6) if (tpad // BLK) % d == 0)
    group = max(d for d in range(1, 9) if bd % d == 0)
    gla_group_p = max(d for d in range(1, 5) if bp % d == 0)
    gla_group_s = max(d for d in range(1, 5) if bd % d == 0)
    cache_k4 = jnp.transpose(cache_k, (0, 1, 3, 4, 2)).reshape(depth, n_pool, SBW, page)
    cache_v4 = jnp.transpose(cache_v, (0, 1, 3, 4, 2)).reshape(depth, n_pool, SBW, page)
    zero_conv = jnp.zeros((bp, CPAD, CCH), F32)
    zero_gla = jnp.zeros((bp, GKW, GVW), F32)

    cp_l, sp_l, ks_l, vs_l, cs_l, ss_l = ([] for _ in range(6))
    kt_all = vt_all = None
    for l in range(depth):
        bias2 = sb_bias[l] * LOG2E
        bias_rows_p = jnp.tile(jnp.repeat(bias2.reshape(NH // 2, 2), BLK, axis=1), (1, nsub))
        bias_pairs = jnp.pad(jnp.stack(_split3(bias_rows_p), axis=-1), ((0, 0), (0, 0), (0, LANE - N_BIAS_PIECES)))
        bias_rows = jnp.broadcast_to(jnp.repeat(bias2, dseq)[:, None], (NH * dseq, page))
        last = l == depth - 1
        u_p, q_p, kt_all, vt_all, kvt_p = _inproj_prompt(xp, norm_g3, w_in_bf, wkv_t, kt_all, vt_all, l, bp, tpad,
                                                         t_real, depth, KBLK)
        mix_sb = _attn_prompt(q_p, kvt_p, u_p, OFF_SBG, bias_pairs, uo_p, bp, tpad, nsub)
        mix_cv, cst_p = _conv(u_p, OFF_SBG, zero_conv, dw_w, dw_b, ln_g, ln_b, pw2_bf, l, bp, tpad, t_real, conv_tb,
                              BF16)
        mix_gla, gst_p = _gla(u_p, OFF_SBG, zero_gla, w2h, w2l, gkb, ng_tiled, ts, gmat, l, bp, tpad, t_real, BLK,
                              gla_group_p, BF16)
        xp = _outproj(xp, mix_sb, mix_cv, mix_gla, w_out_bf, l, bm_p, final_g if last else None)
        cp_l.append(cst_p[:, CPAD - (CW - 1):])
        sp_l.append(_blockdiag_to_state(gst_p))
        u_s = _inproj(xs, norm_g3, w_in_bf, l, bm_s)
        smix_sb = _attn_sample(u_s, cache_k4, cache_v4, page_table, bias_rows, uo, l, bd, dseq, group)
        st_s = jnp.pad(state_conv[l], ((0, 0), (CPAD - (CW - 1), 0), (0, 0)))
        smix_cv, cst_s = _conv(u_s, 0, st_s, dw_w, dw_b, ln_g, ln_b, pw2_bf, l, bd, dseq, dseq, dseq, F32)
        smix_gla, gst_s = _gla(u_s, 0, _state_to_blockdiag(state_gla[l]), w2h, w2l, gkb, ng_tiled, ts, gmat, l, bd,
                               dseq, dseq, dseq, gla_group_s, F32)
        xs = _outproj(xs, smix_sb, smix_cv, smix_gla, w_out_bf, l, bm_s, final_g if last else None)
        ks_l.append(u_s[:, OFF_K:OFF_V].reshape(bd, dseq, NH, DH))
        vs_l.append(u_s[:, OFF_V:OFF_SBG].reshape(bd, dseq, NH, DH))
        cs_l.append(cst_s[:, CPAD - (CW - 1):])
        ss_l.append(_blockdiag_to_state(gst_s))

    y_prompt = xp.reshape(bp, tpad, D_MODEL)[:, N_META:t_real]
    y_sample = xs.reshape(bd, dseq, D_MODEL)
    new_k_prompt = jnp.transpose(kt_all[..., :t_real].reshape(depth, bp, NH, DH, t_real), (0, 1, 4, 2, 3))
    new_v_prompt = jnp.transpose(vt_all[..., :t_real].reshape(depth, bp, NH, DH, t_real), (0, 1, 4, 2, 3))
    return (y_prompt, y_sample, new_k_prompt, new_v_prompt, jnp.stack(cp_l), jnp.stack(sp_l),
            jnp.stack(ks_l), jnp.stack(vs_l), jnp.stack(cs_l), jnp.stack(ss_l))
```

```python
import functools

import jax
import jax.numpy as jnp
from jax import lax
from jax.experimental import pallas as pl
from jax.experimental.pallas import tpu as pltpu

F32 = jnp.float32
BF16 = jnp.bfloat16

D_MODEL = 1024
N_META = 16
NH = 8
DH = 64
SBW = NH * DH
CCH = 256
CW = 31
GH = 4
GDK = 32
GDV = 64
GKW = GH * GDK
GVW = GH * GDV
GRANK = 16
GATE_NORM = 16.0
EPS = 1e-6

LANE = 128
MXU_DIM = 256
BLK = 128
KBLK = MXU_DIM
LOG2E = 1.4426950408889634
Q_SCALE = DH ** -0.5 * LOG2E
N_BIAS_PIECES = 3
GSUB = 32
CPAD = 32
IN_W = 3600
IN_WP = 3712
OFF_Q, OFF_K, OFF_V, OFF_SBG = 0, 512, 1024, 1536
OFF_CIN, OFF_CG = 2048, 2560
OFF_GQ, OFF_GK, OFF_GV, OFF_GG, OFF_GLR = 2816, 2944, 3072, 3328, 3584
VMEM_LIMIT = 48 * 1024 * 1024
ATTN_VMEM_LIMIT = 56 * 1024 * 1024


def _dot(a, b):
    return jnp.dot(a, b, preferred_element_type=F32)


def _dot_nt(a, b):
    return lax.dot_general(a, b, (((1,), (1,)), ((), ())), preferred_element_type=F32)


def _split2(x):
    hi = x.astype(BF16)
    lo = (x - hi.astype(F32)).astype(BF16)
    return hi, lo


def _split3(x):
    hi = x.astype(BF16)
    r = x - hi.astype(F32)
    mid = r.astype(BF16)
    lo = (r - mid.astype(F32)).astype(BF16)
    return hi, mid, lo


def _log_sigmoid(x):
    return jnp.minimum(x, 0.0) - jnp.log(1.0 + jnp.exp(-jnp.abs(x)))


def _silu(x):
    return x * jax.nn.sigmoid(x)


def _params(sem):
    return pltpu.CompilerParams(dimension_semantics=sem, vmem_limit_bytes=VMEM_LIMIT)


def _inproj_kernel(x_ref, g_ref, w_ref, u_ref):
    x = x_ref[...]
    ms = jnp.mean(x * x, axis=-1, keepdims=True)
    h = (x * lax.rsqrt(ms + EPS) * g_ref[...]).astype(BF16)
    for c0 in range(0, IN_WP, SBW):
        wd = min(SBW, IN_WP - c0)
        u_ref[:, c0:c0 + wd] = _dot(h, w_ref[:, c0:c0 + wd])


def _inproj_prompt_kernel(x_ref, g_ref, w_ref, wkv_ref, *rest):
    u_ref, q_ref, kt_ref, vt_ref, kvbf_ref = rest[-5:]
    x = x_ref[...]
    ms = jnp.mean(x * x, axis=-1, keepdims=True)
    h = (x * lax.rsqrt(ms + EPS) * g_ref[...]).astype(BF16)
    q_ref[...] = (_dot(h, w_ref[:, OFF_Q:OFF_Q + SBW]) * Q_SCALE).astype(BF16)
    for c0 in range(OFF_SBG, IN_WP, SBW):
        wd = min(SBW, IN_WP - c0)
        u_ref[:, c0 - OFF_SBG:c0 - OFF_SBG + wd] = _dot(h, w_ref[:, c0:c0 + wd])
    kvt = _dot_nt(wkv_ref[...], h)
    kt_ref[...] = kvt[:SBW]
    vt_ref[...] = kvt[SBW:]
    kvbf_ref[...] = kvt.astype(BF16)


def _inproj_prompt(x, norm_g, w_in_bf, wkv_t, kt_prev, vt_prev, layer, nseq, tpad, t_real, depth, bm):
    n = x.shape[0]
    nt = tpad // bm
    in_specs = [
        pl.BlockSpec((bm, D_MODEL), lambda b, t: (b * nt + t, 0)),
        pl.BlockSpec((None, 1, D_MODEL), lambda b, t: (layer, 0, 0)),
        pl.BlockSpec((None, D_MODEL, IN_WP), lambda b, t: (layer, 0, 0)),
        pl.BlockSpec((None, 2 * SBW, D_MODEL), lambda b, t: (layer, 0, 0)),
    ]
    args = [x, norm_g, w_in_bf, wkv_t]
    aliases = {}
    if kt_prev is not None:
        in_specs += [pl.BlockSpec(memory_space=pl.ANY), pl.BlockSpec(memory_space=pl.ANY)]
        args += [kt_prev, vt_prev]
        aliases = {4: 2, 5: 3}
    cache = jax.ShapeDtypeStruct((depth, nseq, SBW, tpad), F32)
    cache_spec = pl.BlockSpec((None, None, SBW, bm), lambda b, t: (layer, b, 0, t))
    return pl.pallas_call(
        _inproj_prompt_kernel,
        grid=(nseq, nt),
        in_specs=in_specs,
        out_specs=[
            pl.BlockSpec((bm, IN_WP - OFF_SBG), lambda b, t: (b * nt + t, 0)),
            pl.BlockSpec((bm, SBW), lambda b, t: (b * nt + t, 0)),
            cache_spec, cache_spec,
            pl.BlockSpec((None, 2 * SBW, bm), lambda b, t: (b, 0, t)),
        ],
        out_shape=[
            jax.ShapeDtypeStruct((n, IN_WP - OFF_SBG), F32),
            jax.ShapeDtypeStruct((n, SBW), BF16),
            cache, cache,
            jax.ShapeDtypeStruct((nseq, 2 * SBW, tpad), BF16),
        ],
        input_output_aliases=aliases,
        compiler_params=_params(("parallel", "parallel")),
        name="inproj_prompt",
    )(*args)


def _inproj(x, norm_g, w_in_bf, layer, bm):
    n = x.shape[0]
    return pl.pallas_call(
        _inproj_kernel,
        grid=(n // bm,),
        in_specs=[
            pl.BlockSpec((bm, D_MODEL), lambda i: (i, 0)),
            pl.BlockSpec((None, 1, D_MODEL), lambda i: (layer, 0, 0)),
            pl.BlockSpec((None, D_MODEL, IN_WP), lambda i: (layer, 0, 0)),
        ],
        out_specs=pl.BlockSpec((bm, IN_WP), lambda i: (i, 0)),
        out_shape=jax.ShapeDtypeStruct((n, IN_WP), F32),
        compiler_params=_params(("parallel",)),
        name="inproj",
    )(x, norm_g, w_in_bf)


def _outproj_kernel(x_ref, a_ref, b_ref, c_ref, w_ref, *rest, final):
    y = x_ref[...]
    y = y + _dot(a_ref[...].astype(BF16), w_ref[0:SBW, :])
    y = y + _dot(b_ref[...].astype(BF16), w_ref[SBW:SBW + CCH, :])
    y = y + _dot(c_ref[...].astype(BF16), w_ref[SBW + CCH:, :])
    if final:
        g_ref, y_ref = rest
        ms = jnp.mean(y * y, axis=-1, keepdims=True)
        y_ref[...] = y * lax.rsqrt(ms + EPS) * g_ref[...]
    else:
        (y_ref,) = rest
        y_ref[...] = y


def _outproj(x, mix_sb, mix_cv, mix_gla, w_out_bf, layer, bm, final_g=None):
    n = x.shape[0]
    final = final_g is not None
    in_specs = [
        pl.BlockSpec((bm, D_MODEL), lambda i: (i, 0)),
        pl.BlockSpec((bm, SBW), lambda i: (i, 0)),
        pl.BlockSpec((bm, CCH), lambda i: (i, 0)),
        pl.BlockSpec((bm, GVW), lambda i: (i, 0)),
        pl.BlockSpec((None, D_MODEL, D_MODEL), lambda i: (layer, 0, 0)),
    ]
    args = [x, mix_sb, mix_cv, mix_gla, w_out_bf]
    if final:
        in_specs.append(pl.BlockSpec((1, D_MODEL), lambda i: (0, 0)))
        args.append(final_g)
    return pl.pallas_call(
        functools.partial(_outproj_kernel, final=final),
        grid=(n // bm,),
        in_specs=in_specs,
        out_specs=pl.BlockSpec((bm, D_MODEL), lambda i: (i, 0)),
        out_shape=jax.ShapeDtypeStruct((n, D_MODEL), F32),
        compiler_params=_params(("parallel",)),
        name="outproj",
    )(*args)


def _neg_abs(x):
    bits = lax.bitcast_convert_type(x, jnp.uint32) | jnp.uint32(0x80000000)
    return lax.bitcast_convert_type(bits, F32)


def _sb_logs(z):
    l = jnp.log(1.0 + jnp.exp2(_neg_abs(z))) * LOG2E
    ls = jnp.minimum(z, 0.0) - l
    return ls, ls - z


def _lane_tile(x, n):
    reps = n // x.shape[1]
    return x if reps == 1 else jnp.concatenate([x] * reps, axis=1)


def _sb_weights(ls, cr, carry, mask):
    n = ls.shape[1]
    w = jnp.exp2(ls + cr[:, :n] + _lane_tile(carry, n))
    if mask is not None:
        w = jnp.where(mask, w, 0.0)
    return w.astype(BF16)


def _sb_block(qs, kb, vb, uo, carry_ref, acc_ref, mask, r0=0):
    ls, lf = _sb_logs(_dot(qs, kb))
    if mask is not None:
        lf = jnp.where(mask, lf, 0.0)
    lfb = lf.astype(BF16)
    cr = _dot(lfb, uo)
    acc_ref[r0:, :] += _dot_nt(_sb_weights(ls, cr, carry_ref[r0:, :], mask), vb)
    tot = cr[:, 0:1] + lfb[:, 0:1].astype(F32)
    carry_ref[r0:, :] += jnp.broadcast_to(tot, (tot.shape[0], LANE))


def _sb_blocks_staged(qs, kb, vb, bias, uo, carry_ref, acc_ref, mask, transposed):
    m = qs[0].shape[0]
    n = uo.shape[0]
    logs = [_sb_logs((_dot(q, k) if transposed else _dot_nt(q, k)) + bias) for q, k in zip(qs, kb)]
    lfs = [(lf if mask is None else jnp.where(mask, lf, 0.0)).astype(BF16) for _, lf in logs]
    cr_all = _dot(jnp.concatenate(lfs, axis=0), uo)
    for g in range(len(qs)):
        cr = cr_all[g * m:(g + 1) * m]
        w = _sb_weights(logs[g][0], cr, carry_ref[g], mask)
        acc_ref[g] += _dot_nt(w, vb[g]) if transposed else _dot(w, vb[g])
        carry_ref[g] += cr[:, n:]


def _sample_unit(j, q_ref, kn_ref, vn_ref, gate_ref, kp_refs, vp_refs, bias_ref, uo_ref, o_ref, qs_ref, acc_ref,
                 carry_ref, *, n_pages, s):
    group = len(kp_refs)
    m = NH * s
    page = uo_ref.shape[0]
    gi = j // (n_pages + 1)
    ps = j - gi * (n_pages + 1)

    def seq_rows(g):
        return pl.ds(pl.multiple_of((gi * group + g) * s, s), s)

    @pl.when(ps == 0)
    def _():
        head = lax.broadcasted_iota(jnp.int32, (s, SBW), 1) // DH
        row = lax.rem(lax.broadcasted_iota(jnp.int32, (m, page), 0), s)
        col = lax.broadcasted_iota(jnp.int32, (m, page), 1)
        pad = jnp.zeros((page - s, SBW), F32)
        acc_ref[...] = jnp.zeros_like(acc_ref)
        carry_ref[...] = jnp.zeros_like(carry_ref)
        qs, kb, vb = [], [], []
        for g in range(group):
            q = q_ref[seq_rows(g), :] * Q_SCALE
            qs.append(jnp.concatenate([jnp.where(head == h, q, 0.0) for h in range(NH)], axis=0).astype(BF16))
            qs_ref[g] = qs[g]
            kb.append(jnp.concatenate([kn_ref[seq_rows(g), :], pad], axis=0).astype(BF16))
            vb.append(jnp.concatenate([vn_ref[seq_rows(g), :], pad], axis=0).astype(BF16))
        _sb_blocks_staged(qs, kb, vb, bias_ref[...], uo_ref[...], carry_ref, acc_ref, col < row, False)

    @pl.when(ps > 0)
    def _():
        _sb_blocks_staged([qs_ref[g] for g in range(group)],
                          [kp_refs[g][...].astype(BF16) for g in range(group)],
                          [vp_refs[g][...].astype(BF16) for g in range(group)],
                          bias_ref[...], uo_ref[...], carry_ref, acc_ref, None, True)

    @pl.when(ps == n_pages)
    def _():
        head = lax.broadcasted_iota(jnp.int32, (s, SBW), 1) // DH
        for g in range(group):
            o = jnp.zeros((s, SBW), F32)
            for h in range(NH):
                o = jnp.where(head == h, acc_ref[g, h * s:(h + 1) * s, :], o)
            o_ref[seq_rows(g), :] = o * _silu(gate_ref[seq_rows(g), :])


def _attn_kernel(pt_ref, q_ref, k_ref, v_ref, gate_ref, bias_ref, uo_ref, sq_ref, skn_ref, svn_ref, sgate_ref, *rest,
                 nsub, n_pages, group, units, n_units, s):
    del pt_ref
    npg = units * group
    kp_refs, vp_refs = rest[:npg], rest[npg:2 * npg]
    sbias_ref, suo_ref, o_ref, so_ref, qs_ref, acc_ref, carry_ref, sqs_ref, sacc_ref, scarry_ref = rest[2 * npg:]
    step = (pl.program_id(0) * pl.num_programs(1) + pl.program_id(1)) * pl.num_programs(2) + pl.program_id(2)
    for un in range(units):
        j = step * units + un

        @pl.when(j < n_units)
        def _(j=j, un=un):
            _sample_unit(j, sq_ref, skn_ref, svn_ref, sgate_ref, kp_refs[un * group:(un + 1) * group],
                         vp_refs[un * group:(un + 1) * group], sbias_ref, suo_ref, so_ref, sqs_ref, sacc_ref,
                         scarry_ref, n_pages=n_pages, s=s)

    qi = pl.program_id(2)
    qb = nsub * BLK
    lane = lax.broadcasted_iota(jnp.int32, (BLK, LANE), 1)
    for g in range(nsub):
        q = q_ref[g * BLK:(g + 1) * BLK, :]
        zero = jnp.zeros_like(q)
        qs_ref[2 * g * BLK:(2 * g + 1) * BLK, :LANE] = jnp.where(lane < DH, q, zero)
        qs_ref[(2 * g + 1) * BLK:(2 * g + 2) * BLK, :LANE] = jnp.where(lane >= DH, q, zero)
    qs_ref[:, LANE:] = bias_ref[...]
    ones_rows = jnp.where(lax.broadcasted_iota(jnp.int32, (LANE, KBLK), 0) < N_BIAS_PIECES, 1.0, 0.0).astype(BF16)
    acc_ref[...] = jnp.zeros_like(acc_ref)
    carry_ref[...] = jnp.zeros_like(carry_ref)
    uo = uo_ref[...]
    for c in reversed(range(qb // KBLK)):
        r0 = 2 * c * KBLK
        m = 2 * qb - r0
        row = lax.broadcasted_iota(jnp.int32, (m, KBLK), 0)
        col = lax.broadcasted_iota(jnp.int32, (m, KBLK), 1)
        first_hidden = (row // (2 * BLK)) * BLK + (row & (BLK - 1))
        mask = col < jnp.where(row >= 2 * KBLK, KBLK, first_hidden)
        ks = pl.multiple_of(qi * qb + c * KBLK, KBLK)
        kb = jnp.concatenate([k_ref[:, pl.ds(ks, KBLK)], ones_rows], axis=0)
        _sb_block(qs_ref[r0:, :], kb, v_ref[:, pl.ds(ks, KBLK)], uo, carry_ref, acc_ref, mask, r0=r0)

    def body(t, carry):
        for c in reversed(range(qb // KBLK)):
            ks = pl.multiple_of((qi - 1 - t) * qb + c * KBLK, KBLK)
            kb = jnp.concatenate([k_ref[:, pl.ds(ks, KBLK)], ones_rows], axis=0)
            _sb_block(qs_ref[...], kb, v_ref[:, pl.ds(ks, KBLK)], uo, carry_ref, acc_ref, None)
        return carry

    lax.fori_loop(0, qi, body, 0)
    for g in range(nsub):
        o = jnp.where(lane < DH, acc_ref[2 * g * BLK:(2 * g + 1) * BLK, :],
                      acc_ref[(2 * g + 1) * BLK:(2 * g + 2) * BLK, :])
        o_ref[g * BLK:(g + 1) * BLK, :] = (o * _silu(gate_ref[g * BLK:(g + 1) * BLK, :])).astype(o_ref.dtype)


def _attn(q, kvt, u, ubase, bias_pairs, uo, nseq, tpad, nsub, u_s, cache_k, cache_v, page_table, bias_rows, uo_s,
          layer, nseq_s, s, group):
    n = nseq * tpad
    qb = nsub * BLK
    nq = tpad // qb
    hp = NH // 2
    gcol = (OFF_SBG - ubase) // LANE
    n_pages = page_table.shape[1]
    page = cache_k.shape[3]
    n_units = (nseq_s // group) * (n_pages + 1)
    n_steps = nseq * hp * nq
    units = -(-n_units // n_steps)
    ns = nseq_s * s

    def page_map(un, g):
        def index(b, h, i, pt):
            j = jnp.minimum(((b * hp + h) * nq + i) * units + un, n_units - 1)
            gi = j // (n_pages + 1)
            ps = j - gi * (n_pages + 1)
            return (layer, pt[gi * group + g, n_pages - jnp.maximum(ps, 1)], 0, 0)
        return index

    page_specs = [pl.BlockSpec((None, None, SBW, page), page_map(un, g)) for un in range(units) for g in range(group)]

    def sample_cols(off):
        return pl.BlockSpec((ns, SBW), lambda b, h, i, pt: (0, off // SBW))

    grid_spec = pltpu.PrefetchScalarGridSpec(
        num_scalar_prefetch=1,
        grid=(nseq, hp, nq),
        in_specs=[
            pl.BlockSpec((qb, LANE), lambda b, h, i, pt: (b * nq + i, h)),
            pl.BlockSpec((None, LANE, tpad), lambda b, h, i, pt: (b, h, 0)),
            pl.BlockSpec((None, LANE, tpad), lambda b, h, i, pt: (b, hp + h, 0)),
            pl.BlockSpec((qb, LANE), lambda b, h, i, pt: (b * nq + i, gcol + h)),
            pl.BlockSpec((None, 2 * qb, LANE), lambda b, h, i, pt: (h, 0, 0)),
            pl.BlockSpec((KBLK, KBLK), lambda b, h, i, pt: (0, 0)),
            sample_cols(OFF_Q), sample_cols(OFF_K), sample_cols(OFF_V), sample_cols(OFF_SBG),
            *page_specs, *page_specs,
            pl.BlockSpec((NH * s, page), lambda b, h, i, pt: (0, 0)),
            pl.BlockSpec((page, 2 * page), lambda b, h, i, pt: (0, 0)),
        ],
        out_specs=[
            pl.BlockSpec((qb, LANE), lambda b, h, i, pt: (b * nq + i, h)),
            pl.BlockSpec((ns, SBW), lambda b, h, i, pt: (0, 0)),
        ],
        scratch_shapes=[pltpu.VMEM((2 * qb, 2 * LANE), BF16), pltpu.VMEM((2 * qb, LANE), F32),
                        pltpu.VMEM((2 * qb, LANE), F32),
                        pltpu.VMEM((group, NH * s, SBW), BF16), pltpu.VMEM((group, NH * s, SBW), F32),
                        pltpu.VMEM((group, NH * s, page), F32)],
    )
    return pl.pallas_call(
        functools.partial(_attn_kernel, nsub=nsub, n_pages=n_pages, group=group, units=units, n_units=n_units, s=s),
        grid_spec=grid_spec,
        out_shape=[jax.ShapeDtypeStruct((n, SBW), BF16), jax.ShapeDtypeStruct((ns, SBW), F32)],
        compiler_params=pltpu.CompilerParams(dimension_semantics=("arbitrary", "arbitrary", "arbitrary"),
                                             vmem_limit_bytes=ATTN_VMEM_LIMIT),
        name="attn",
    )(page_table, q, kvt, kvt, u, bias_pairs, uo, u_s, u_s, u_s, u_s,
      *([cache_k] * (units * group)), *([cache_v] * (units * group)), bias_rows, uo_s)


def _conv_kernel(cin_ref, cg_ref, st_ref, w_ref, b_ref, lng_ref, lnb_ref, pw_ref, o_ref, stout_ref,
                 full_ref, shift_ref, y_ref, *, tb, sub, last_t, state_off):
    t = pl.program_id(1)

    @pl.when(t == 0)
    def _():
        full_ref[0:CPAD, :] = st_ref[...]

    cin = cin_ref[...]
    full_ref[CPAD:CPAD + tb, :] = cin[:, :CCH] * jax.nn.sigmoid(cin[:, CCH:])
    nshift = shift_ref.shape[1]
    for s in range(1, 8):
        shift_ref[s - 1] = full_ref[s:s + nshift, :]
    lead = CPAD - (CW - 1)
    for r0 in range(0, tb, sub):
        acc = jnp.zeros((sub, CCH), F32) + b_ref[...]
        for j in range(CW):
            a, s = divmod(lead + j, 8)
            lo = r0 + 8 * a
            rows = full_ref[lo:lo + sub, :] if s == 0 else shift_ref[s - 1, lo:lo + sub, :]
            acc = acc + jnp.concatenate([w_ref[8 * j:8 * j + 8, :]] * (sub // 8), axis=0) * rows
        y_ref[r0:r0 + sub, :] = acc
    c = y_ref[...]
    mu = jnp.mean(c, axis=-1, keepdims=True)
    d = c - mu
    var = jnp.mean(d * d, axis=-1, keepdims=True)
    c = d * lax.rsqrt(var + EPS) * lng_ref[...] + lnb_ref[...]
    c = _dot(_silu(c).astype(BF16), pw_ref[...])
    o_ref[...] = (c * _silu(cg_ref[...])).astype(o_ref.dtype)

    @pl.when(t == last_t)
    def _():
        stout_ref[...] = full_ref[state_off:state_off + CPAD, :]

    full_ref[0:CPAD, :] = full_ref[tb:tb + CPAD, :]


def _conv(u, ubase, state_pad, dw_w, dw_b, ln_g, ln_b, pw2_bf, layer, nseq, tpad, t_real, tb, out_dtype):
    nt = tpad // tb
    last_t = (t_real - 1) // tb
    state_off = t_real - last_t * tb
    sub = min(tb, 32)
    kern = functools.partial(_conv_kernel, tb=tb, sub=sub, last_t=last_t, state_off=state_off)
    vec = pl.BlockSpec((None, 1, CCH), lambda b, t: (layer, 0, 0))
    return pl.pallas_call(
        kern,
        grid=(nseq, nt),
        in_specs=[
            pl.BlockSpec((tb, 2 * CCH), lambda b, t: (b * nt + t, (OFF_CIN - ubase) // (2 * CCH))),
            pl.BlockSpec((tb, CCH), lambda b, t: (b * nt + t, (OFF_CG - ubase) // CCH)),
            pl.BlockSpec((None, CPAD, CCH), lambda b, t: (b, 0, 0)),
            pl.BlockSpec((None, 8 * CW, CCH), lambda b, t: (layer, 0, 0)),
            vec, vec, vec,
            pl.BlockSpec((None, CCH, CCH), lambda b, t: (layer, 0, 0)),
        ],
        out_specs=[
            pl.BlockSpec((tb, CCH), lambda b, t: (b * nt + t, 0)),
            pl.BlockSpec((None, CPAD, CCH), lambda b, t: (b, 0, 0)),
        ],
        out_shape=[
            jax.ShapeDtypeStruct((nseq * tpad, CCH), out_dtype),
            jax.ShapeDtypeStruct((nseq, CPAD, CCH), F32),
        ],
        scratch_shapes=[pltpu.VMEM((CPAD + tb, CCH), F32), pltpu.VMEM((7, tb + CPAD - 8, CCH), F32),
                        pltpu.VMEM((tb, CCH), F32)],
        compiler_params=_params(("arbitrary", "arbitrary")),
        name="conv",
    )(u, u, state_pad, dw_w, dw_b, ln_g, ln_b, pw2_bf)


def _gla_kernel(gq_ref, gk_ref, gv_ref, gg_ref, glr_ref, s0_ref, w2h_ref, w2l_ref, gkb_ref, ng_ref,
                ts_ref, gmat_ref, o_ref, sout_ref, s_ref, *, rows, t_real, n_chunks):
    c = pl.program_id(1)
    seqs = range(gq_ref.shape[0])

    @pl.when(c == 0)
    def _():
        s_ref[...] = s0_ref[...]

    def padrows(x):
        if rows == BLK:
            return x
        return jnp.concatenate([x, jnp.zeros((BLK - rows, x.shape[1]), x.dtype)], axis=0)

    def rows_of(x, g):
        return x[g * BLK:(g + 1) * BLK]

    ridx = lax.broadcasted_iota(jnp.int32, (BLK, GKW), 0)
    valid = ridx < jnp.minimum(rows, t_real - c * rows)
    valid_all = jnp.concatenate([valid] * len(seqs), axis=0)
    gq = [padrows(gq_ref[g]) * (GDK ** -0.5) for g in seqs]
    gk = [jnp.where(valid, padrows(gk_ref[g]), 0.0) for g in seqs]
    gv_bf = [padrows(gv_ref[g]).astype(BF16) for g in seqs]
    glr_h, glr_l = _split2(jnp.concatenate([padrows(glr_ref[g]) for g in seqs], axis=0))
    x = _dot(glr_h, w2h_ref[...]) + _dot(glr_l, w2h_ref[...]) + _dot(glr_h, w2l_ref[...]) + gkb_ref[...]
    lg_all = jnp.where(valid_all, _log_sigmoid(x) * (1.0 / GATE_NORM), 0.0)
    lg = [rows_of(lg_all, g) for g in seqs]
    ts = ts_ref[...]
    lg_h, lg_m, lg_l = _split3(jnp.concatenate(lg, axis=1))
    br = _dot(ts, lg_h) + _dot(ts, lg_m) + _dot(ts, lg_l)
    b = [br[:BLK, g * GKW:(g + 1) * GKW] for g in seqs]
    r = [br[BLK:, g * GKW:(g + 1) * GKW] for g in seqs]
    s_bd = [s_ref[g] for g in seqs]
    o_inter = [_dot((gq[g] * jnp.exp(b[g])).astype(BF16), s_bd[g].astype(BF16)) for g in seqs]
    qd = [gq[g] * jnp.exp(b[g] - r[g]) for g in seqs]
    nst = GH * GSUB
    qmask = (lax.broadcasted_iota(jnp.int32, (nst, GKW), 1) // GDK) == (lax.broadcasted_iota(jnp.int32, (nst, GKW), 0) // GSUB)
    vmask = (lax.broadcasted_iota(jnp.int32, (nst, GVW), 1) // GDV) == (lax.broadcasted_iota(jnp.int32, (nst, GVW), 0) // GSUB)
    trow = lax.rem(lax.broadcasted_iota(jnp.int32, (nst, BLK), 0), GSUB)
    scol = lax.broadcasted_iota(jnp.int32, (nst, BLK), 1)
    nsc = BLK // GSUB
    att = [[] for _ in seqs]
    for i in range(nsc):
        lo = i * GSUB
        kd = [(gk[g] * jnp.exp(r[g][lo:lo + 1, :] - b[g])).astype(BF16) for g in seqs]
        qst = [jnp.where(qmask, jnp.concatenate([qd[g][lo:lo + GSUB]] * GH, axis=0), 0.0).astype(BF16) for g in seqs]
        for g in seqs:
            att[g].append(jnp.where(scol <= trow + lo, _dot_nt(qst[g], kd[g]), 0.0).astype(BF16))
    vmask_all = jnp.concatenate([vmask] * nsc, axis=0)
    ov = [jnp.where(vmask_all, _dot(jnp.concatenate(att[g], axis=0), gv_bf[g]), 0.0) for g in seqs]

    def heads_summed(x, i):
        parts = [x[(i * GH + h) * GSUB:(i * GH + h + 1) * GSUB] for h in range(GH)]
        return functools.reduce(lambda acc, part: acc + part, parts)

    o = jnp.concatenate([jnp.concatenate([heads_summed(ov[g], i) for i in range(nsc)], axis=0) + o_inter[g]
                         for g in seqs], axis=0)
    o2_h, o2_l = _split2(o * o)
    ms = _dot(o2_h, gmat_ref[...]) + _dot(o2_l, gmat_ref[...])
    gate = jnp.concatenate([padrows(gg_ref[g]) for g in seqs], axis=0)
    res = o * lax.rsqrt(ms + EPS) * ng_ref[...] * _silu(gate)
    for g in seqs:
        o_ref[g] = rows_of(res, g)[:rows].astype(o_ref.dtype)
    kdec_t = [(gk[g] * jnp.exp(b[g][BLK - 1:BLK, :] - b[g])).T.astype(BF16) for g in seqs]
    kv = [_dot(kdec_t[g], gv_bf[g]) for g in seqs]
    bdmask = (lax.broadcasted_iota(jnp.int32, (GKW, GVW), 0) // GDK) == (lax.broadcasted_iota(jnp.int32, (GKW, GVW), 1) // GDV)
    for g in seqs:
        bl_col = jnp.sum(lg[g].T, axis=1, keepdims=True)
        s_ref[g] = jnp.exp(bl_col) * s_bd[g] + jnp.where(bdmask, kv[g], 0.0)

    @pl.when(c == n_chunks - 1)
    def _():
        sout_ref[...] = s_ref[...]


def _gla(u, ubase, s0_bd, w2h, w2l, gkb, ng_tiled, ts, gmat, layer, nseq, tpad, t_real, rows, group, out_dtype):
    nc = tpad // rows
    u3 = u.reshape(nseq, tpad, u.shape[1])
    kern = functools.partial(_gla_kernel, rows=rows, t_real=t_real, n_chunks=nc)

    def cols(off, width):
        return pl.BlockSpec((group, rows, width), lambda b, c: (b, c, (off - ubase) // width))

    state_spec = pl.BlockSpec((group, GKW, GVW), lambda b, c: (b, 0, 0))
    o, s_out = pl.pallas_call(
        kern,
        grid=(nseq // group, nc),
        in_specs=[
            cols(OFF_GQ, GKW), cols(OFF_GK, GKW), cols(OFF_GV, GVW), cols(OFF_GG, GVW), cols(OFF_GLR, LANE),
            state_spec,
            pl.BlockSpec((None, LANE, GKW), lambda b, c: (layer, 0, 0)),
            pl.BlockSpec((None, LANE, GKW), lambda b, c: (layer, 0, 0)),
            pl.BlockSpec((None, 1, GKW), lambda b, c: (layer, 0, 0)),
            pl.BlockSpec((None, 1, GVW), lambda b, c: (layer, 0, 0)),
            pl.BlockSpec((2 * BLK, BLK), lambda b, c: (0, 0)),
            pl.BlockSpec((GVW, GVW), lambda b, c: (0, 0)),
        ],
        out_specs=[
            pl.BlockSpec((group, rows, GVW), lambda b, c: (b, c, 0)),
            state_spec,
        ],
        out_shape=[
            jax.ShapeDtypeStruct((nseq, tpad, GVW), out_dtype),
            jax.ShapeDtypeStruct((nseq, GKW, GVW), F32),
        ],
        scratch_shapes=[pltpu.VMEM((group, GKW, GVW), F32)],
        compiler_params=_params(("arbitrary", "arbitrary")),
        name="gla",
    )(u3, u3, u3, u3, u3, s0_bd, w2h, w2l, gkb, ng_tiled, ts, gmat)
    return o.reshape(nseq * tpad, GVW), s_out


def _state_to_blockdiag(s):
    n = s.shape[0]
    out = jnp.zeros((n, GH, GDK, GH, GDV), F32)
    for h in range(GH):
        out = out.at[:, h, :, h, :].set(s[:, h].astype(F32))
    return out.reshape(n, GKW, GVW)


def _blockdiag_to_state(s_bd):
    n = s_bd.shape[0]
    s5 = s_bd.reshape(n, GH, GDK, GH, GDV)
    return jnp.stack([s5[:, h, :, h, :] for h in range(GH)], axis=1)


def _row_block(n):
    for bm in (256, 128, 64, 32, 16, 8):
        if n % bm == 0:
            return bm
    raise ValueError(f"row count {n} is not a multiple of 8")


def kernel(x_prompt, x_sample, cache_k, cache_v, state_conv, state_gla, page_table, meta_tokens, norm_g, w_in,
           sb_bias, conv_dw_w, conv_dw_b, conv_ln_g, conv_ln_b, conv_pw2, gla_gk_w2, gla_gk_b, gla_norm_g, w_out,
           final_norm_g):
    bp, seq, _ = x_prompt.shape
    bd, dseq, _ = x_sample.shape
    depth = w_in.shape[0]
    n_pool, page = cache_k.shape[1], cache_k.shape[2]
    assert page == BLK and dseq % 8 == 0 and BLK % dseq == 0
    t_real = N_META + seq
    tpad = -(-t_real // KBLK) * KBLK

    w_in_bf = jnp.pad(w_in, ((0, 0), (0, 0), (0, IN_WP - IN_W))).astype(BF16)
    wkv_t = jnp.transpose(w_in[:, :, OFF_K:OFF_SBG], (0, 2, 1)).astype(BF16)
    w_out_bf = w_out.astype(BF16)
    norm_g3 = norm_g.reshape(depth, 1, D_MODEL)
    dw_w = jnp.repeat(conv_dw_w, 8, axis=1)
    dw_b = conv_dw_b.reshape(depth, 1, CCH)
    ln_g = conv_ln_g.reshape(depth, 1, CCH)
    ln_b = conv_ln_b.reshape(depth, 1, CCH)
    pw2_bf = conv_pw2.astype(BF16)
    w2 = jnp.pad(gla_gk_w2, ((0, 0), (0, LANE - GRANK), (0, 0)))
    w2h = w2.astype(BF16)
    w2l = (w2 - w2h.astype(F32)).astype(BF16)
    gkb = gla_gk_b.reshape(depth, 1, GKW)
    ng_tiled = jnp.tile(gla_norm_g, (1, GH)).reshape(depth, 1, GVW)
    ii = jnp.arange(BLK)
    uo = jnp.concatenate([(ii[:, None] > ii[None, :]), jnp.ones((BLK, LANE), bool)], axis=1).astype(BF16)
    ik = jnp.arange(KBLK)
    uo_p = (ik[:, None] > ik[None, :]).astype(BF16)
    tri = ii[None, :] <= ii[:, None]
    sel = ii[None, :] < (ii[:, None] // GSUB) * GSUB
    ts = jnp.concatenate([tri, sel], axis=0).astype(BF16)
    gi = jnp.arange(GVW) // GDV
    gmat = ((gi[:, None] == gi[None, :]).astype(F32) / GDV).astype(BF16)
    final_g = final_norm_g.reshape(1, D_MODEL)

    meta = jnp.broadcast_to(meta_tokens[None].astype(F32), (bp, N_META, D_MODEL))
    xp = jnp.concatenate([meta, x_prompt, jnp.zeros((bp, tpad - t_real, D_MODEL), F32)], axis=1)
    xp = xp.reshape(bp * tpad, D_MODEL)
    xs = x_sample.reshape(bd * dseq, D_MODEL)
    bm_p = _row_block(bp * tpad)
    bm_s = _row_block(bd * dseq)
    conv_tb = KBLK if tpad % KBLK == 0 else BLK
    nsub = max(d for d in (2, 4, 6) if (tpad // BLK) % d == 0)
    group = max(d for d in range(1, 9) if bd % d == 0)
    gla_group_p = max(d for d in range(1, 5) if bp % d == 0)
    gla_group_s = max(d for d in range(1, 5) if bd % d == 0)
    cache_k4 = jnp.transpose(cache_k, (0, 1, 3, 4, 2)).reshape(depth, n_pool, SBW, page)
    cache_v4 = jnp.transpose(cache_v, (0, 1, 3, 4, 2)).reshape(depth, n_pool, SBW, page)
    zero_conv = jnp.zeros((bp, CPAD, CCH), F32)
    zero_gla = jnp.zeros((bp, GKW, GVW), F32)

    cp_l, sp_l, ks_l, vs_l, cs_l, ss_l = ([] for _ in range(6))
    kt_all = vt_all = None
    for l in range(depth):
        bias2 = sb_bias[l] * LOG2E
        bias_rows_p = jnp.tile(jnp.repeat(bias2.reshape(NH // 2, 2), BLK, axis=1), (1, nsub))
        bias_pairs = jnp.pad(jnp.stack(_split3(bias_rows_p), axis=-1), ((0, 0), (0, 0), (0, LANE - N_BIAS_PIECES)))
        bias_rows = jnp.broadcast_to(jnp.repeat(bias2, dseq)[:, None], (NH * dseq, page))
        last = l == depth - 1
        u_p, q_p, kt_all, vt_all, kvt_p = _inproj_prompt(xp, norm_g3, w_in_bf, wkv_t, kt_all, vt_all, l, bp, tpad,
                                                         t_real, depth, KBLK)
        u_s = _inproj(xs, norm_g3, w_in_bf, l, bm_s)
        mix_sb, smix_sb = _attn(q_p, kvt_p, u_p, OFF_SBG, bias_pairs, uo_p, bp, tpad, nsub, u_s, cache_k4, cache_v4,
                                page_table, bias_rows, uo, l, bd, dseq, group)
        mix_cv, cst_p = _conv(u_p, OFF_SBG, zero_conv, dw_w, dw_b, ln_g, ln_b, pw2_bf, l, bp, tpad, t_real, conv_tb,
                              BF16)
        mix_gla, gst_p = _gla(u_p, OFF_SBG, zero_gla, w2h, w2l, gkb, ng_tiled, ts, gmat, l, bp, tpad, t_real, BLK,
                              gla_group_p, BF16)
        xp = _outproj(xp, mix_sb, mix_cv, mix_gla, w_out_bf, l, bm_p, final_g if last else None)
        cp_l.append(cst_p[:, CPAD - (CW - 1):])
        sp_l.append(_blockdiag_to_state(gst_p))
        st_s = jnp.pad(state_conv[l], ((0, 0), (CPAD - (CW - 1), 0), (0, 0)))
        smix_cv, cst_s = _conv(u_s, 0, st_s, dw_w, dw_b, ln_g, ln_b, pw2_bf, l, bd, dseq, dseq, dseq, F32)
        smix_gla, gst_s = _gla(u_s, 0, _state_to_blockdiag(state_gla[l]), w2h, w2l, gkb, ng_tiled, ts, gmat, l, bd,
                               dseq, dseq, dseq, gla_group_s, F32)
        xs = _outproj(xs, smix_sb, smix_cv, smix_gla, w_out_bf, l, bm_s, final_g if last else None)
        ks_l.append(u_s[:, OFF_K:OFF_V].reshape(bd, dseq, NH, DH))
        vs_l.append(u_s[:, OFF_V:OFF_SBG].reshape(bd, dseq, NH, DH))
        cs_l.append(cst_s[:, CPAD - (CW - 1):])
        ss_l.append(_blockdiag_to_state(gst_s))

    y_prompt = xp.reshape(bp, tpad, D_MODEL)[:, N_META:t_real]
    y_sample = xs.reshape(bd, dseq, D_MODEL)
    new_k_prompt = jnp.transpose(kt_all[..., :t_real].reshape(depth, bp, NH, DH, t_real), (0, 1, 4, 2, 3))
    new_v_prompt = jnp.transpose(vt_all[..., :t_real].reshape(depth, bp, NH, DH, t_real), (0, 1, 4, 2, 3))
    return (y_prompt, y_sample, new_k_prompt, new_v_prompt, jnp.stack(cp_l), jnp.stack(sp_l),
            jnp.stack(ks_l), jnp.stack(vs_l), jnp.stack(cs_l), jnp.stack(ss_l))
```

```python
import functools

import jax
import jax.numpy as jnp
from jax import lax
from jax.experimental import pallas as pl
from jax.experimental.pallas import tpu as pltpu

F32 = jnp.float32
BF16 = jnp.bfloat16

D_MODEL = 1024
N_META = 16
NH = 8
DH = 64
SBW = NH * DH
CCH = 256
CW = 31
GH = 4
GDK = 32
GDV = 64
GKW = GH * GDK
GVW = GH * GDV
GRANK = 16
GATE_NORM = 16.0
EPS = 1e-6

LANE = 128
MXU_DIM = 256
BLK = 128
KBLK = MXU_DIM
LOG2E = 1.4426950408889634
Q_SCALE = DH ** -0.5 * LOG2E
N_BIAS_PIECES = 3
GSUB = 32
CPAD = 32
IN_W = 3600
IN_WP = 3712
OFF_Q, OFF_K, OFF_V, OFF_SBG = 0, 512, 1024, 1536
OFF_CIN, OFF_CG = 2048, 2560
OFF_GQ, OFF_GK, OFF_GV, OFF_GG, OFF_GLR = 2816, 2944, 3072, 3328, 3584
VMEM_LIMIT = 48 * 1024 * 1024
ATTN_VMEM_LIMIT = 56 * 1024 * 1024


def _dot(a, b):
    return jnp.dot(a, b, preferred_element_type=F32)


def _dot_nt(a, b):
    return lax.dot_general(a, b, (((1,), (1,)), ((), ())), preferred_element_type=F32)


def _split2(x):
    hi = x.astype(BF16)
    lo = (x - hi.astype(F32)).astype(BF16)
    return hi, lo


def _split3(x):
    hi = x.astype(BF16)
    r = x - hi.astype(F32)
    mid = r.astype(BF16)
    lo = (r - mid.astype(F32)).astype(BF16)
    return hi, mid, lo


def _log_sigmoid(x):
    return jnp.minimum(x, 0.0) - jnp.log(1.0 + jnp.exp(-jnp.abs(x)))


def _silu(x):
    return x * jax.nn.sigmoid(x)


def _params(sem):
    return pltpu.CompilerParams(dimension_semantics=sem, vmem_limit_bytes=VMEM_LIMIT)


def _inproj_kernel(x_ref, g_ref, w_ref, u_ref):
    x = x_ref[...]
    ms = jnp.mean(x * x, axis=-1, keepdims=True)
    h = (x * lax.rsqrt(ms + EPS) * g_ref[...]).astype(BF16)
    for c0 in range(0, IN_WP, SBW):
        wd = min(SBW, IN_WP - c0)
        u_ref[:, c0:c0 + wd] = _dot(h, w_ref[:, c0:c0 + wd])


def _inproj_prompt_kernel(x_ref, g_ref, w_ref, wkv_ref, *rest):
    u_ref, q_ref, kt_ref, vt_ref, kvbf_ref = rest[-5:]
    x = x_ref[...]
    ms = jnp.mean(x * x, axis=-1, keepdims=True)
    h = (x * lax.rsqrt(ms + EPS) * g_ref[...]).astype(BF16)
    q_ref[...] = (_dot(h, w_ref[:, OFF_Q:OFF_Q + SBW]) * Q_SCALE).astype(BF16)
    for c0 in range(OFF_SBG, IN_WP, SBW):
        wd = min(SBW, IN_WP - c0)
        u_ref[:, c0 - OFF_SBG:c0 - OFF_SBG + wd] = _dot(h, w_ref[:, c0:c0 + wd])
    kvt = _dot_nt(wkv_ref[...], h)
    kt_ref[...] = kvt[:SBW]
    vt_ref[...] = kvt[SBW:]
    kvbf_ref[...] = kvt.astype(BF16)


def _inproj_prompt(x, norm_g, w_in_bf, wkv_t, kt_prev, vt_prev, layer, nseq, tpad, t_real, depth, bm):
    n = x.shape[0]
    nt = tpad // bm
    in_specs = [
        pl.BlockSpec((bm, D_MODEL), lambda b, t: (b * nt + t, 0)),
        pl.BlockSpec((None, 1, D_MODEL), lambda b, t: (layer, 0, 0)),
        pl.BlockSpec((None, D_MODEL, IN_WP), lambda b, t: (layer, 0, 0)),
        pl.BlockSpec((None, 2 * SBW, D_MODEL), lambda b, t: (layer, 0, 0)),
    ]
    args = [x, norm_g, w_in_bf, wkv_t]
    aliases = {}
    if kt_prev is not None:
        in_specs += [pl.BlockSpec(memory_space=pl.ANY), pl.BlockSpec(memory_space=pl.ANY)]
        args += [kt_prev, vt_prev]
        aliases = {4: 2, 5: 3}
    cache = jax.ShapeDtypeStruct((depth, nseq, SBW, tpad), F32)
    cache_spec = pl.BlockSpec((None, None, SBW, bm), lambda b, t: (layer, b, 0, t))
    return pl.pallas_call(
        _inproj_prompt_kernel,
        grid=(nseq, nt),
        in_specs=in_specs,
        out_specs=[
            pl.BlockSpec((bm, IN_WP - OFF_SBG), lambda b, t: (b * nt + t, 0)),
            pl.BlockSpec((bm, SBW), lambda b, t: (b * nt + t, 0)),
            cache_spec, cache_spec,
            pl.BlockSpec((None, 2 * SBW, bm), lambda b, t: (b, 0, t)),
        ],
        out_shape=[
            jax.ShapeDtypeStruct((n, IN_WP - OFF_SBG), F32),
            jax.ShapeDtypeStruct((n, SBW), BF16),
            cache, cache,
            jax.ShapeDtypeStruct((nseq, 2 * SBW, tpad), BF16),
        ],
        input_output_aliases=aliases,
        compiler_params=_params(("parallel", "parallel")),
        name="inproj_prompt",
    )(*args)


def _inproj(x, norm_g, w_in_bf, layer, bm):
    n = x.shape[0]
    return pl.pallas_call(
        _inproj_kernel,
        grid=(n // bm,),
        in_specs=[
            pl.BlockSpec((bm, D_MODEL), lambda i: (i, 0)),
            pl.BlockSpec((None, 1, D_MODEL), lambda i: (layer, 0, 0)),
            pl.BlockSpec((None, D_MODEL, IN_WP), lambda i: (layer, 0, 0)),
        ],
        out_specs=pl.BlockSpec((bm, IN_WP), lambda i: (i, 0)),
        out_shape=jax.ShapeDtypeStruct((n, IN_WP), F32),
        compiler_params=_params(("parallel",)),
        name="inproj",
    )(x, norm_g, w_in_bf)


def _outproj_kernel(x_ref, a_ref, b_ref, c_ref, w_ref, *rest, final):
    y = x_ref[...]
    y = y + _dot(a_ref[...].astype(BF16), w_ref[0:SBW, :])
    y = y + _dot(b_ref[...].astype(BF16), w_ref[SBW:SBW + CCH, :])
    y = y + _dot(c_ref[...].astype(BF16), w_ref[SBW + CCH:, :])
    if final:
        g_ref, y_ref = rest
        ms = jnp.mean(y * y, axis=-1, keepdims=True)
        y_ref[...] = y * lax.rsqrt(ms + EPS) * g_ref[...]
    else:
        (y_ref,) = rest
        y_ref[...] = y


def _outproj(x, mix_sb, mix_cv, mix_gla, w_out_bf, layer, bm, final_g=None):
    n = x.shape[0]
    final = final_g is not None
    in_specs = [
        pl.BlockSpec((bm, D_MODEL), lambda i: (i, 0)),
        pl.BlockSpec((bm, SBW), lambda i: (i, 0)),
        pl.BlockSpec((bm, CCH), lambda i: (i, 0)),
        pl.BlockSpec((bm, GVW), lambda i: (i, 0)),
        pl.BlockSpec((None, D_MODEL, D_MODEL), lambda i: (layer, 0, 0)),
    ]
    args = [x, mix_sb, mix_cv, mix_gla, w_out_bf]
    if final:
        in_specs.append(pl.BlockSpec((1, D_MODEL), lambda i: (0, 0)))
        args.append(final_g)
    return pl.pallas_call(
        functools.partial(_outproj_kernel, final=final),
        grid=(n // bm,),
        in_specs=in_specs,
        out_specs=pl.BlockSpec((bm, D_MODEL), lambda i: (i, 0)),
        out_shape=jax.ShapeDtypeStruct((n, D_MODEL), F32),
        compiler_params=_params(("parallel",)),
        name="outproj",
    )(*args)


def _neg_abs(x):
    bits = lax.bitcast_convert_type(x, jnp.uint32) | jnp.uint32(0x80000000)
    return lax.bitcast_convert_type(bits, F32)


def _sb_logs(z):
    l = jnp.log(1.0 + jnp.exp2(_neg_abs(z))) * LOG2E
    ls = jnp.minimum(z, 0.0) - l
    return ls, ls - z


def _lane_tile(x, n):
    reps = n // x.shape[1]
    return x if reps == 1 else jnp.concatenate([x] * reps, axis=1)


def _sb_weights(ls, cr, carry, mask):
    n = ls.shape[1]
    w = jnp.exp2(ls + cr[:, :n] + _lane_tile(carry, n))
    if mask is not None:
        w = jnp.where(mask, w, 0.0)
    return w.astype(BF16)


def _sb_block(qs, kb, vb, uo, carry_ref, acc_ref, mask, r0=0):
    ls, lf = _sb_logs(_dot(qs, kb))
    if mask is not None:
        lf = jnp.where(mask, lf, 0.0)
    lfb = lf.astype(BF16)
    cr = _dot(lfb, uo)
    acc_ref[r0:, :] += _dot_nt(_sb_weights(ls, cr, carry_ref[r0:, :], mask), vb)
    tot = cr[:, 0:1] + lfb[:, 0:1].astype(F32)
    carry_ref[r0:, :] += jnp.broadcast_to(tot, (tot.shape[0], LANE))


def _sb_blocks_staged(qs, kb, vb, bias, uo, carry_ref, acc_ref, mask, transposed):
    m = qs[0].shape[0]
    n = uo.shape[0]
    logs = [_sb_logs((_dot(q, k) if transposed else _dot_nt(q, k)) + bias) for q, k in zip(qs, kb)]
    lfs = [(lf if mask is None else jnp.where(mask, lf, 0.0)).astype(BF16) for _, lf in logs]
    cr_all = _dot(jnp.concatenate(lfs, axis=0), uo)
    for g in range(len(qs)):
        cr = cr_all[g * m:(g + 1) * m]
        w = _sb_weights(logs[g][0], cr, carry_ref[g], mask)
        acc_ref[g] += _dot_nt(w, vb[g]) if transposed else _dot(w, vb[g])
        carry_ref[g] += cr[:, n:]


def _sample_unit(j, q_ref, kn_ref, vn_ref, gate_ref, kp_refs, vp_refs, bias_ref, uo_ref, o_ref, qs_ref, acc_ref,
                 carry_ref, *, n_pages, s):
    group = len(kp_refs)
    m = NH * s
    page = uo_ref.shape[0]
    gi = j // (n_pages + 1)
    ps = j - gi * (n_pages + 1)

    def seq_rows(g):
        return pl.ds(pl.multiple_of((gi * group + g) * s, s), s)

    @pl.when(ps == 0)
    def _():
        head = lax.broadcasted_iota(jnp.int32, (s, SBW), 1) // DH
        row = lax.rem(lax.broadcasted_iota(jnp.int32, (m, page), 0), s)
        col = lax.broadcasted_iota(jnp.int32, (m, page), 1)
        pad = jnp.zeros((page - s, SBW), F32)
        acc_ref[...] = jnp.zeros_like(acc_ref)
        carry_ref[...] = jnp.zeros_like(carry_ref)
        qs, kb, vb = [], [], []
        for g in range(group):
            q = q_ref[seq_rows(g), :] * Q_SCALE
            qs.append(jnp.concatenate([jnp.where(head == h, q, 0.0) for h in range(NH)], axis=0).astype(BF16))
            qs_ref[g] = qs[g]
            kb.append(jnp.concatenate([kn_ref[seq_rows(g), :], pad], axis=0).astype(BF16))
            vb.append(jnp.concatenate([vn_ref[seq_rows(g), :], pad], axis=0).astype(BF16))
        _sb_blocks_staged(qs, kb, vb, bias_ref[...], uo_ref[...], carry_ref, acc_ref, col < row, False)

    @pl.when(ps > 0)
    def _():
        _sb_blocks_staged([qs_ref[g] for g in range(group)],
                          [kp_refs[g][...].astype(BF16) for g in range(group)],
                          [vp_refs[g][...].astype(BF16) for g in range(group)],
                          bias_ref[...], uo_ref[...], carry_ref, acc_ref, None, True)

    @pl.when(ps == n_pages)
    def _():
        head = lax.broadcasted_iota(jnp.int32, (s, SBW), 1) // DH
        for g in range(group):
            o = jnp.zeros((s, SBW), F32)
            for h in range(NH):
                o = jnp.where(head == h, acc_ref[g, h * s:(h + 1) * s, :], o)
            o_ref[seq_rows(g), :] = o * _silu(gate_ref[seq_rows(g), :])


def _attn_kernel(pt_ref, q_ref, k_ref, v_ref, gate_ref, bias_ref, uo_ref, sq_ref, skn_ref, svn_ref, sgate_ref, *rest,
                 nsub, n_pages, group, units, n_units, s):
    del pt_ref
    npg = units * group
    kp_refs, vp_refs = rest[:npg], rest[npg:2 * npg]
    sbias_ref, suo_ref, o_ref, so_ref, qs_ref, acc_ref, carry_ref, sqs_ref, sacc_ref, scarry_ref = rest[2 * npg:]
    step = (pl.program_id(0) * pl.num_programs(1) + pl.program_id(1)) * pl.num_programs(2) + pl.program_id(2)
    for un in range(units):
        j = step * units + un

        @pl.when(j < n_units)
        def _(j=j, un=un):
            _sample_unit(j, sq_ref, skn_ref, svn_ref, sgate_ref, kp_refs[un * group:(un + 1) * group],
                         vp_refs[un * group:(un + 1) * group], sbias_ref, suo_ref, so_ref, sqs_ref, sacc_ref,
                         scarry_ref, n_pages=n_pages, s=s)

    qi = pl.program_id(2)
    qb = nsub * BLK
    lane = lax.broadcasted_iota(jnp.int32, (BLK, LANE), 1)
    for g in range(nsub):
        q = q_ref[g * BLK:(g + 1) * BLK, :]
        zero = jnp.zeros_like(q)
        qs_ref[2 * g * BLK:(2 * g + 1) * BLK, :LANE] = jnp.where(lane < DH, q, zero)
        qs_ref[(2 * g + 1) * BLK:(2 * g + 2) * BLK, :LANE] = jnp.where(lane >= DH, q, zero)
    qs_ref[:, LANE:] = bias_ref[...]
    ones_rows = jnp.where(lax.broadcasted_iota(jnp.int32, (LANE, KBLK), 0) < N_BIAS_PIECES, 1.0, 0.0).astype(BF16)
    acc_ref[...] = jnp.zeros_like(acc_ref)
    carry_ref[...] = jnp.zeros_like(carry_ref)
    uo = uo_ref[...]
    for c in reversed(range(qb // KBLK)):
        r0 = 2 * c * KBLK
        m = 2 * qb - r0
        row = lax.broadcasted_iota(jnp.int32, (m, KBLK), 0)
        col = lax.broadcasted_iota(jnp.int32, (m, KBLK), 1)
        first_hidden = (row // (2 * BLK)) * BLK + (row & (BLK - 1))
        mask = col < jnp.where(row >= 2 * KBLK, KBLK, first_hidden)
        ks = pl.multiple_of(qi * qb + c * KBLK, KBLK)
        kb = jnp.concatenate([k_ref[:, pl.ds(ks, KBLK)], ones_rows], axis=0)
        _sb_block(qs_ref[r0:, :], kb, v_ref[:, pl.ds(ks, KBLK)], uo, carry_ref, acc_ref, mask, r0=r0)

    def older_span(kbase):
        for c in reversed(range(qb // KBLK)):
            ks = pl.multiple_of(kbase + c * KBLK, KBLK)
            kb = jnp.concatenate([k_ref[:, pl.ds(ks, KBLK)], ones_rows], axis=0)
            _sb_block(qs_ref[...], kb, v_ref[:, pl.ds(ks, KBLK)], uo, carry_ref, acc_ref, None)

    def body(t, carry):
        kbase = (qi - 1 - 2 * t) * qb
        older_span(kbase)
        older_span(kbase - qb)
        return carry

    lax.fori_loop(0, lax.shift_right_logical(qi, 1), body, 0)

    @pl.when((qi & 1) == 1)
    def _():
        older_span(0)

    for g in range(nsub):
        o = jnp.where(lane < DH, acc_ref[2 * g * BLK:(2 * g + 1) * BLK, :],
                      acc_ref[(2 * g + 1) * BLK:(2 * g + 2) * BLK, :])
        o_ref[g * BLK:(g + 1) * BLK, :] = (o * _silu(gate_ref[g * BLK:(g + 1) * BLK, :])).astype(o_ref.dtype)


def _attn(q, kvt, u, ubase, bias_pairs, uo, nseq, tpad, nsub, u_s, cache_k, cache_v, page_table, bias_rows, uo_s,
          layer, nseq_s, s, group):
    n = nseq * tpad
    qb = nsub * BLK
    nq = tpad // qb
    hp = NH // 2
    gcol = (OFF_SBG - ubase) // LANE
    n_pages = page_table.shape[1]
    page = cache_k.shape[3]
    n_units = (nseq_s // group) * (n_pages + 1)
    n_steps = nseq * hp * nq
    units = -(-n_units // n_steps)
    ns = nseq_s * s

    j = jnp.minimum(jnp.arange(n_steps)[:, None, None] * units + jnp.arange(units)[None, :, None], n_units - 1)
    gi = j // (n_pages + 1)
    ps = j - gi * (n_pages + 1)
    step_pages = page_table[gi * group + jnp.arange(group)[None, None, :], n_pages - jnp.maximum(ps, 1)]
    step_pages = step_pages.reshape(n_steps, units * group)

    def page_map(un, g):
        return lambda b, h, i, pt: (layer, pt[(b * hp + h) * nq + i, un * group + g], 0, 0)

    page_specs = [pl.BlockSpec((None, None, SBW, page), page_map(un, g)) for un in range(units) for g in range(group)]

    def sample_cols(off):
        return pl.BlockSpec((ns, SBW), lambda b, h, i, pt: (0, off // SBW))

    grid_spec = pltpu.PrefetchScalarGridSpec(
        num_scalar_prefetch=1,
        grid=(nseq, hp, nq),
        in_specs=[
            pl.BlockSpec((qb, LANE), lambda b, h, i, pt: (b * nq + i, h)),
            pl.BlockSpec((None, LANE, tpad), lambda b, h, i, pt: (b, h, 0)),
            pl.BlockSpec((None, LANE, tpad), lambda b, h, i, pt: (b, hp + h, 0)),
            pl.BlockSpec((qb, LANE), lambda b, h, i, pt: (b * nq + i, gcol + h)),
            pl.BlockSpec((None, 2 * qb, LANE), lambda b, h, i, pt: (h, 0, 0)),
            pl.BlockSpec((KBLK, KBLK), lambda b, h, i, pt: (0, 0)),
            sample_cols(OFF_Q), sample_cols(OFF_K), sample_cols(OFF_V), sample_cols(OFF_SBG),
            *page_specs, *page_specs,
            pl.BlockSpec((NH * s, page), lambda b, h, i, pt: (0, 0)),
            pl.BlockSpec((page, 2 * page), lambda b, h, i, pt: (0, 0)),
        ],
        out_specs=[
            pl.BlockSpec((qb, LANE), lambda b, h, i, pt: (b * nq + i, h)),
            pl.BlockSpec((ns, SBW), lambda b, h, i, pt: (0, 0)),
        ],
        scratch_shapes=[pltpu.VMEM((2 * qb, 2 * LANE), BF16), pltpu.VMEM((2 * qb, LANE), F32),
                        pltpu.VMEM((2 * qb, LANE), F32),
                        pltpu.VMEM((group, NH * s, SBW), BF16), pltpu.VMEM((group, NH * s, SBW), F32),
                        pltpu.VMEM((group, NH * s, page), F32)],
    )
    return pl.pallas_call(
        functools.partial(_attn_kernel, nsub=nsub, n_pages=n_pages, group=group, units=units, n_units=n_units, s=s),
        grid_spec=grid_spec,
        out_shape=[jax.ShapeDtypeStruct((n, SBW), BF16), jax.ShapeDtypeStruct((ns, SBW), F32)],
        compiler_params=pltpu.CompilerParams(dimension_semantics=("arbitrary", "arbitrary", "arbitrary"),
                                             vmem_limit_bytes=ATTN_VMEM_LIMIT),
        name="attn",
    )(step_pages, q, kvt, kvt, u, bias_pairs, uo, u_s, u_s, u_s, u_s,
      *([cache_k] * (units * group)), *([cache_v] * (units * group)), bias_rows, uo_s)


def _conv_kernel(cin_ref, cg_ref, st_ref, w_ref, b_ref, lng_ref, lnb_ref, pw_ref, o_ref, stout_ref,
                 full_ref, shift_ref, y_ref, *, tb, sub, last_t, state_off):
    t = pl.program_id(1)

    @pl.when(t == 0)
    def _():
        full_ref[0:CPAD, :] = st_ref[...]

    cin = cin_ref[...]
    full_ref[CPAD:CPAD + tb, :] = cin[:, :CCH] * jax.nn.sigmoid(cin[:, CCH:])
    nshift = shift_ref.shape[1]
    for s in range(1, 8):
        shift_ref[s - 1] = full_ref[s:s + nshift, :]
    lead = CPAD - (CW - 1)
    for r0 in range(0, tb, sub):
        acc = jnp.zeros((sub, CCH), F32) + b_ref[...]
        for j in range(CW):
            a, s = divmod(lead + j, 8)
            lo = r0 + 8 * a
            rows = full_ref[lo:lo + sub, :] if s == 0 else shift_ref[s - 1, lo:lo + sub, :]
            acc = acc + jnp.concatenate([w_ref[8 * j:8 * j + 8, :]] * (sub // 8), axis=0) * rows
        y_ref[r0:r0 + sub, :] = acc
    c = y_ref[...]
    mu = jnp.mean(c, axis=-1, keepdims=True)
    d = c - mu
    var = jnp.mean(d * d, axis=-1, keepdims=True)
    c = d * lax.rsqrt(var + EPS) * lng_ref[...] + lnb_ref[...]
    c = _dot(_silu(c).astype(BF16), pw_ref[...])
    o_ref[...] = (c * _silu(cg_ref[...])).astype(o_ref.dtype)

    @pl.when(t == last_t)
    def _():
        stout_ref[...] = full_ref[state_off:state_off + CPAD, :]

    full_ref[0:CPAD, :] = full_ref[tb:tb + CPAD, :]


def _conv(u, ubase, state_pad, dw_w, dw_b, ln_g, ln_b, pw2_bf, layer, nseq, tpad, t_real, tb, out_dtype):
    nt = tpad // tb
    last_t = (t_real - 1) // tb
    state_off = t_real - last_t * tb
    sub = min(tb, 32)
    kern = functools.partial(_conv_kernel, tb=tb, sub=sub, last_t=last_t, state_off=state_off)
    vec = pl.BlockSpec((None, 1, CCH), lambda b, t: (layer, 0, 0))
    return pl.pallas_call(
        kern,
        grid=(nseq, nt),
        in_specs=[
            pl.BlockSpec((tb, 2 * CCH), lambda b, t: (b * nt + t, (OFF_CIN - ubase) // (2 * CCH))),
            pl.BlockSpec((tb, CCH), lambda b, t: (b * nt + t, (OFF_CG - ubase) // CCH)),
            pl.BlockSpec((None, CPAD, CCH), lambda b, t: (b, 0, 0)),
            pl.BlockSpec((None, 8 * CW, CCH), lambda b, t: (layer, 0, 0)),
            vec, vec, vec,
            pl.BlockSpec((None, CCH, CCH), lambda b, t: (layer, 0, 0)),
        ],
        out_specs=[
            pl.BlockSpec((tb, CCH), lambda b, t: (b * nt + t, 0)),
            pl.BlockSpec((None, CPAD, CCH), lambda b, t: (b, 0, 0)),
        ],
        out_shape=[
            jax.ShapeDtypeStruct((nseq * tpad, CCH), out_dtype),
            jax.ShapeDtypeStruct((nseq, CPAD, CCH), F32),
        ],
        scratch_shapes=[pltpu.VMEM((CPAD + tb, CCH), F32), pltpu.VMEM((7, tb + CPAD - 8, CCH), F32),
                        pltpu.VMEM((tb, CCH), F32)],
        compiler_params=_params(("arbitrary", "arbitrary")),
        name="conv",
    )(u, u, state_pad, dw_w, dw_b, ln_g, ln_b, pw2_bf)


def _gla_kernel(gq_ref, gk_ref, gv_ref, gg_ref, glr_ref, s0_ref, w2h_ref, w2l_ref, gkb_ref, ng_ref,
                ts_ref, gmat_ref, o_ref, sout_ref, s_ref, *, rows, t_real, n_chunks):
    c = pl.program_id(1)
    seqs = range(gq_ref.shape[0])

    @pl.when(c == 0)
    def _():
        s_ref[...] = s0_ref[...]

    def padrows(x):
        if rows == BLK:
            return x
        return jnp.concatenate([x, jnp.zeros((BLK - rows, x.shape[1]), x.dtype)], axis=0)

    def rows_of(x, g):
        return x[g * BLK:(g + 1) * BLK]

    ridx = lax.broadcasted_iota(jnp.int32, (BLK, GKW), 0)
    valid = ridx < jnp.minimum(rows, t_real - c * rows)
    valid_all = jnp.concatenate([valid] * len(seqs), axis=0)
    gq = [padrows(gq_ref[g]) * (GDK ** -0.5) for g in seqs]
    gk = [jnp.where(valid, padrows(gk_ref[g]), 0.0) for g in seqs]
    gv_bf = [padrows(gv_ref[g]).astype(BF16) for g in seqs]
    glr_h, glr_l = _split2(jnp.concatenate([padrows(glr_ref[g]) for g in seqs], axis=0))
    x = _dot(glr_h, w2h_ref[...]) + _dot(glr_l, w2h_ref[...]) + _dot(glr_h, w2l_ref[...]) + gkb_ref[...]
    lg_all = jnp.where(valid_all, _log_sigmoid(x) * (1.0 / GATE_NORM), 0.0)
    lg = [rows_of(lg_all, g) for g in seqs]
    ts = ts_ref[...]
    lg_h, lg_m, lg_l = _split3(jnp.concatenate(lg, axis=1))
    br = _dot(ts, lg_h) + _dot(ts, lg_m) + _dot(ts, lg_l)
    b = [br[:BLK, g * GKW:(g + 1) * GKW] for g in seqs]
    r = [br[BLK:, g * GKW:(g + 1) * GKW] for g in seqs]
    s_bd = [s_ref[g] for g in seqs]
    o_inter = [_dot((gq[g] * jnp.exp(b[g])).astype(BF16), s_bd[g].astype(BF16)) for g in seqs]
    qd = [gq[g] * jnp.exp(b[g] - r[g]) for g in seqs]
    nst = GH * GSUB
    qmask = (lax.broadcasted_iota(jnp.int32, (nst, GKW), 1) // GDK) == (lax.broadcasted_iota(jnp.int32, (nst, GKW), 0) // GSUB)
    vmask = (lax.broadcasted_iota(jnp.int32, (nst, GVW), 1) // GDV) == (lax.broadcasted_iota(jnp.int32, (nst, GVW), 0) // GSUB)
    trow = lax.rem(lax.broadcasted_iota(jnp.int32, (nst, BLK), 0), GSUB)
    scol = lax.broadcasted_iota(jnp.int32, (nst, BLK), 1)
    nsc = BLK // GSUB
    att = [[] for _ in seqs]
    for i in range(nsc):
        lo = i * GSUB
        kd = [(gk[g] * jnp.exp(r[g][lo:lo + 1, :] - b[g])).astype(BF16) for g in seqs]
        qst = [jnp.where(qmask, jnp.concatenate([qd[g][lo:lo + GSUB]] * GH, axis=0), 0.0).astype(BF16) for g in seqs]
        for g in seqs:
            att[g].append(jnp.where(scol <= trow + lo, _dot_nt(qst[g], kd[g]), 0.0).astype(BF16))
    vmask_all = jnp.concatenate([vmask] * nsc, axis=0)
    ov = [jnp.where(vmask_all, _dot(jnp.concatenate(att[g], axis=0), gv_bf[g]), 0.0) for g in seqs]

    def heads_summed(x, i):
        parts = [x[(i * GH + h) * GSUB:(i * GH + h + 1) * GSUB] for h in range(GH)]
        return functools.reduce(lambda acc, part: acc + part, parts)

    o = jnp.concatenate([jnp.concatenate([heads_summed(ov[g], i) for i in range(nsc)], axis=0) + o_inter[g]
                         for g in seqs], axis=0)
    o2_h, o2_l = _split2(o * o)
    ms = _dot(o2_h, gmat_ref[...]) + _dot(o2_l, gmat_ref[...])
    gate = jnp.concatenate([padrows(gg_ref[g]) for g in seqs], axis=0)
    res = o * lax.rsqrt(ms + EPS) * ng_ref[...] * _silu(gate)
    for g in seqs:
        o_ref[g] = rows_of(res, g)[:rows].astype(o_ref.dtype)
    kdec_t = [(gk[g] * jnp.exp(b[g][BLK - 1:BLK, :] - b[g])).T.astype(BF16) for g in seqs]
    kv = [_dot(kdec_t[g], gv_bf[g]) for g in seqs]
    bdmask = (lax.broadcasted_iota(jnp.int32, (GKW, GVW), 0) // GDK) == (lax.broadcasted_iota(jnp.int32, (GKW, GVW), 1) // GDV)
    for g in seqs:
        bl_col = jnp.sum(lg[g].T, axis=1, keepdims=True)
        s_ref[g] = jnp.exp(bl_col) * s_bd[g] + jnp.where(bdmask, kv[g], 0.0)

    @pl.when(c == n_chunks - 1)
    def _():
        sout_ref[...] = s_ref[...]


def _gla(u, ubase, s0_bd, w2h, w2l, gkb, ng_tiled, ts, gmat, layer, nseq, tpad, t_real, rows, group, out_dtype):
    nc = tpad // rows
    u3 = u.reshape(nseq, tpad, u.shape[1])
    kern = functools.partial(_gla_kernel, rows=rows, t_real=t_real, n_chunks=nc)

    def cols(off, width):
        return pl.BlockSpec((group, rows, width), lambda b, c: (b, c, (off - ubase) // width))

    state_spec = pl.BlockSpec((group, GKW, GVW), lambda b, c: (b, 0, 0))
    o, s_out = pl.pallas_call(
        kern,
        grid=(nseq // group, nc),
        in_specs=[
            cols(OFF_GQ, GKW), cols(OFF_GK, GKW), cols(OFF_GV, GVW), cols(OFF_GG, GVW), cols(OFF_GLR, LANE),
            state_spec,
            pl.BlockSpec((None, LANE, GKW), lambda b, c: (layer, 0, 0)),
            pl.BlockSpec((None, LANE, GKW), lambda b, c: (layer, 0, 0)),
            pl.BlockSpec((None, 1, GKW), lambda b, c: (layer, 0, 0)),
            pl.BlockSpec((None, 1, GVW), lambda b, c: (layer, 0, 0)),
            pl.BlockSpec((2 * BLK, BLK), lambda b, c: (0, 0)),
            pl.BlockSpec((GVW, GVW), lambda b, c: (0, 0)),
        ],
        out_specs=[
            pl.BlockSpec((group, rows, GVW), lambda b, c: (b, c, 0)),
            state_spec,
        ],
        out_shape=[
            jax.ShapeDtypeStruct((nseq, tpad, GVW), out_dtype),
            jax.ShapeDtypeStruct((nseq, GKW, GVW), F32),
        ],
        scratch_shapes=[pltpu.VMEM((group, GKW, GVW), F32)],
        compiler_params=_params(("arbitrary", "arbitrary")),
        name="gla",
    )(u3, u3, u3, u3, u3, s0_bd, w2h, w2l, gkb, ng_tiled, ts, gmat)
    return o.reshape(nseq * tpad, GVW), s_out


def _state_to_blockdiag(s):
    n = s.shape[0]
    out = jnp.zeros((n, GH, GDK, GH, GDV), F32)
    for h in range(GH):
        out = out.at[:, h, :, h, :].set(s[:, h].astype(F32))
    return out.reshape(n, GKW, GVW)


def _blockdiag_to_state(s_bd):
    n = s_bd.shape[0]
    s5 = s_bd.reshape(n, GH, GDK, GH, GDV)
    return jnp.stack([s5[:, h, :, h, :] for h in range(GH)], axis=1)


def _row_block(n):
    for bm in (512, 256, 128, 64, 32, 16, 8):
        if n % bm == 0:
            return bm
    raise ValueError(f"row count {n} is not a multiple of 8")


def kernel(x_prompt, x_sample, cache_k, cache_v, state_conv, state_gla, page_table, meta_tokens, norm_g, w_in,
           sb_bias, conv_dw_w, conv_dw_b, conv_ln_g, conv_ln_b, conv_pw2, gla_gk_w2, gla_gk_b, gla_norm_g, w_out,
           final_norm_g):
    bp, seq, _ = x_prompt.shape
    bd, dseq, _ = x_sample.shape
    depth = w_in.shape[0]
    n_pool, page = cache_k.shape[1], cache_k.shape[2]
    assert page == BLK and dseq % 8 == 0 and BLK % dseq == 0
    t_real = N_META + seq
    tpad = -(-t_real // KBLK) * KBLK

    w_in_bf = jnp.pad(w_in, ((0, 0), (0, 0), (0, IN_WP - IN_W))).astype(BF16)
    wkv_t = jnp.transpose(w_in[:, :, OFF_K:OFF_SBG], (0, 2, 1)).astype(BF16)
    w_out_bf = w_out.astype(BF16)
    norm_g3 = norm_g.reshape(depth, 1, D_MODEL)
    dw_w = jnp.repeat(conv_dw_w, 8, axis=1)
    dw_b = conv_dw_b.reshape(depth, 1, CCH)
    ln_g = conv_ln_g.reshape(depth, 1, CCH)
    ln_b = conv_ln_b.reshape(depth, 1, CCH)
    pw2_bf = conv_pw2.astype(BF16)
    w2 = jnp.pad(gla_gk_w2, ((0, 0), (0, LANE - GRANK), (0, 0)))
    w2h = w2.astype(BF16)
    w2l = (w2 - w2h.astype(F32)).astype(BF16)
    gkb = gla_gk_b.reshape(depth, 1, GKW)
    ng_tiled = jnp.tile(gla_norm_g, (1, GH)).reshape(depth, 1, GVW)
    ii = jnp.arange(BLK)
    uo = jnp.concatenate([(ii[:, None] > ii[None, :]), jnp.ones((BLK, LANE), bool)], axis=1).astype(BF16)
    ik = jnp.arange(KBLK)
    uo_p = (ik[:, None] > ik[None, :]).astype(BF16)
    tri = ii[None, :] <= ii[:, None]
    sel = ii[None, :] < (ii[:, None] // GSUB) * GSUB
    ts = jnp.concatenate([tri, sel], axis=0).astype(BF16)
    gi = jnp.arange(GVW) // GDV
    gmat = ((gi[:, None] == gi[None, :]).astype(F32) / GDV).astype(BF16)
    final_g = final_norm_g.reshape(1, D_MODEL)

    meta = jnp.broadcast_to(meta_tokens[None].astype(F32), (bp, N_META, D_MODEL))
    xp = jnp.concatenate([meta, x_prompt, jnp.zeros((bp, tpad - t_real, D_MODEL), F32)], axis=1)
    xp = xp.reshape(bp * tpad, D_MODEL)
    xs = x_sample.reshape(bd * dseq, D_MODEL)
    bm_p = _row_block(bp * tpad)
    bm_s = _row_block(bd * dseq)
    conv_tb = KBLK if tpad % KBLK == 0 else BLK
    nsub = max(d for d in (2, 4, 6) if (tpad // BLK) % d == 0)
    group = max(d for d in range(1, 9) if bd % d == 0)
    gla_group_p = max(d for d in range(1, 5) if bp % d == 0)
    gla_group_s = max(d for d in range(1, 5) if bd % d == 0)
    cache_k4 = jnp.transpose(cache_k, (0, 1, 3, 4, 2)).reshape(depth, n_pool, SBW, page)
    cache_v4 = jnp.transpose(cache_v, (0, 1, 3, 4, 2)).reshape(depth, n_pool, SBW, page)
    zero_conv = jnp.zeros((bp, CPAD, CCH), F32)
    zero_gla = jnp.zeros((bp, GKW, GVW), F32)

    cp_l, sp_l, ks_l, vs_l, cs_l, ss_l = ([] for _ in range(6))
    kt_all = vt_all = None
    for l in range(depth):
        bias2 = sb_bias[l] * LOG2E
        bias_rows_p = jnp.tile(jnp.repeat(bias2.reshape(NH // 2, 2), BLK, axis=1), (1, nsub))
        bias_pairs = jnp.pad(jnp.stack(_split3(bias_rows_p), axis=-1), ((0, 0), (0, 0), (0, LANE - N_BIAS_PIECES)))
        bias_rows = jnp.broadcast_to(jnp.repeat(bias2, dseq)[:, None], (NH * dseq, page))
        last = l == depth - 1
        u_p, q_p, kt_all, vt_all, kvt_p = _inproj_prompt(xp, norm_g3, w_in_bf, wkv_t, kt_all, vt_all, l, bp, tpad,
                                                         t_real, depth, KBLK)
        u_s = _inproj(xs, norm_g3, w_in_bf, l, bm_s)
        mix_sb, smix_sb = _attn(q_p, kvt_p, u_p, OFF_SBG, bias_pairs, uo_p, bp, tpad, nsub, u_s, cache_k4, cache_v4,
                                page_table, bias_rows, uo, l, bd, dseq, group)
        mix_cv, cst_p = _conv(u_p, OFF_SBG, zero_conv, dw_w, dw_b, ln_g, ln_b, pw2_bf, l, bp, tpad, t_real, conv_tb,
                              BF16)
        mix_gla, gst_p = _gla(u_p, OFF_SBG, zero_gla, w2h, w2l, gkb, ng_tiled, ts, gmat, l, bp, tpad, t_real, BLK,
                              gla_group_p, BF16)
        xp = _outproj(xp, mix_sb, mix_cv, mix_gla, w_out_bf, l, bm_p, final_g if last else None)
        cp_l.append(cst_p[:, CPAD - (CW - 1):])
        sp_l.append(_blockdiag_to_state(gst_p))
        st_s = jnp.pad(state_conv[l], ((0, 0), (CPAD - (CW - 1), 0), (0, 0)))
        smix_cv, cst_s = _conv(u_s, 0, st_s, dw_w, dw_b, ln_g, ln_b, pw2_bf, l, bd, dseq, dseq, dseq, F32)
        smix_gla, gst_s = _gla(u_s, 0, _state_to_blockdiag(state_gla[l]), w2h, w2l, gkb, ng_tiled, ts, gmat, l, bd,
                               dseq, dseq, dseq, gla_group_s, F32)
        xs = _outproj(xs, smix_sb, smix_cv, smix_gla, w_out_bf, l, bm_s, final_g if last else None)
        ks_l.append(u_s[:, OFF_K:OFF_V].reshape(bd, dseq, NH, DH))
        vs_l.append(u_s[:, OFF_V:OFF_SBG].reshape(bd, dseq, NH, DH))
        cs_l.append(cst_s[:, CPAD - (CW - 1):])
        ss_l.append(_blockdiag_to_state(gst_s))

    y_prompt = xp.reshape(bp, tpad, D_MODEL)[:, N_META:t_real]
    y_sample = xs.reshape(bd, dseq, D_MODEL)
    new_k_prompt = jnp.transpose(kt_all[..., :t_real].reshape(depth, bp, NH, DH, t_real), (0, 1, 4, 2, 3))
    new_v_prompt = jnp.transpose(vt_all[..., :t_real].reshape(depth, bp, NH, DH, t_real), (0, 1, 4, 2, 3))
    return (y_prompt, y_sample, new_k_prompt, new_v_prompt, jnp.stack(cp_l), jnp.stack(sp_l),
            jnp.stack(ks_l), jnp.stack(vs_l), jnp.stack(cs_l), jnp.stack(ss_l))
```

```python
import functools

import jax
import jax.numpy as jnp
from jax import lax
from jax.experimental import pallas as pl
from jax.experimental.pallas import tpu as pltpu

F32 = jnp.float32
BF16 = jnp.bfloat16

D_MODEL = 1024
N_META = 16
NH = 8
DH = 64
SBW = NH * DH
CCH = 256
CW = 31
GH = 4
GDK = 32
GDV = 64
GKW = GH * GDK
GVW = GH * GDV
GRANK = 16
GATE_NORM = 16.0
EPS = 1e-6

LANE = 128
MXU_DIM = 256
BLK = 128
KBLK = MXU_DIM
SPANS_PER_TRIP = 3
LOG2E = 1.4426950408889634
Q_SCALE = DH ** -0.5 * LOG2E
N_BIAS_PIECES = 3
GSUB = 32
CPAD = 32
IN_W = 3600
IN_WP = 3712
OFF_Q, OFF_K, OFF_V, OFF_SBG = 0, 512, 1024, 1536
OFF_CIN, OFF_CG = 2048, 2560
OFF_GQ, OFF_GK, OFF_GV, OFF_GG, OFF_GLR = 2816, 2944, 3072, 3328, 3584
VMEM_LIMIT = 48 * 1024 * 1024
ATTN_VMEM_LIMIT = 56 * 1024 * 1024


def _dot(a, b):
    return jnp.dot(a, b, preferred_element_type=F32)


def _dot_nt(a, b):
    return lax.dot_general(a, b, (((1,), (1,)), ((), ())), preferred_element_type=F32)


def _split2(x):
    hi = x.astype(BF16)
    lo = (x - hi.astype(F32)).astype(BF16)
    return hi, lo


def _split3(x):
    hi = x.astype(BF16)
    r = x - hi.astype(F32)
    mid = r.astype(BF16)
    lo = (r - mid.astype(F32)).astype(BF16)
    return hi, mid, lo


def _log_sigmoid(x):
    return jnp.minimum(x, 0.0) - jnp.log(1.0 + jnp.exp(-jnp.abs(x)))


def _silu(x):
    return x * jax.nn.sigmoid(x)


def _params(sem):
    return pltpu.CompilerParams(dimension_semantics=sem, vmem_limit_bytes=VMEM_LIMIT)


def _inproj_kernel(x_ref, g_ref, w_ref, u_ref):
    x = x_ref[...]
    ms = jnp.mean(x * x, axis=-1, keepdims=True)
    h = (x * lax.rsqrt(ms + EPS) * g_ref[...]).astype(BF16)
    for c0 in range(0, IN_WP, SBW):
        wd = min(SBW, IN_WP - c0)
        u_ref[:, c0:c0 + wd] = _dot(h, w_ref[:, c0:c0 + wd])


def _inproj_prompt_kernel(x_ref, g_ref, w_ref, wkv_ref, *rest):
    u_ref, q_ref, kt_ref, vt_ref, kvbf_ref = rest[-5:]
    x = x_ref[...]
    ms = jnp.mean(x * x, axis=-1, keepdims=True)
    h = (x * lax.rsqrt(ms + EPS) * g_ref[...]).astype(BF16)
    q_ref[...] = (_dot(h, w_ref[:, OFF_Q:OFF_Q + SBW]) * Q_SCALE).astype(BF16)
    for c0 in range(OFF_SBG, IN_WP, SBW):
        wd = min(SBW, IN_WP - c0)
        u_ref[:, c0 - OFF_SBG:c0 - OFF_SBG + wd] = _dot(h, w_ref[:, c0:c0 + wd])
    kvt = _dot_nt(wkv_ref[...], h)
    kt_ref[...] = kvt[:SBW]
    vt_ref[...] = kvt[SBW:]
    kvbf_ref[...] = kvt.astype(BF16)


def _inproj_prompt(x, norm_g, w_in_bf, wkv_t, kt_prev, vt_prev, layer, nseq, tpad, t_real, depth, bm):
    n = x.shape[0]
    nt = tpad // bm
    in_specs = [
        pl.BlockSpec((bm, D_MODEL), lambda b, t: (b * nt + t, 0)),
        pl.BlockSpec((None, 1, D_MODEL), lambda b, t: (layer, 0, 0)),
        pl.BlockSpec((None, D_MODEL, IN_WP), lambda b, t: (layer, 0, 0)),
        pl.BlockSpec((None, 2 * SBW, D_MODEL), lambda b, t: (layer, 0, 0)),
    ]
    args = [x, norm_g, w_in_bf, wkv_t]
    aliases = {}
    if kt_prev is not None:
        in_specs += [pl.BlockSpec(memory_space=pl.ANY), pl.BlockSpec(memory_space=pl.ANY)]
        args += [kt_prev, vt_prev]
        aliases = {4: 2, 5: 3}
    cache = jax.ShapeDtypeStruct((depth, nseq, SBW, tpad), F32)
    cache_spec = pl.BlockSpec((None, None, SBW, bm), lambda b, t: (layer, b, 0, t))
    return pl.pallas_call(
        _inproj_prompt_kernel,
        grid=(nseq, nt),
        in_specs=in_specs,
        out_specs=[
            pl.BlockSpec((bm, IN_WP - OFF_SBG), lambda b, t: (b * nt + t, 0)),
            pl.BlockSpec((bm, SBW), lambda b, t: (b * nt + t, 0)),
            cache_spec, cache_spec,
            pl.BlockSpec((None, 2 * SBW, bm), lambda b, t: (b, 0, t)),
        ],
        out_shape=[
            jax.ShapeDtypeStruct((n, IN_WP - OFF_SBG), F32),
            jax.ShapeDtypeStruct((n, SBW), BF16),
            cache, cache,
            jax.ShapeDtypeStruct((nseq, 2 * SBW, tpad), BF16),
        ],
        input_output_aliases=aliases,
        compiler_params=_params(("parallel", "parallel")),
        name="inproj_prompt",
    )(*args)


def _inproj(x, norm_g, w_in_bf, layer, bm):
    n = x.shape[0]
    return pl.pallas_call(
        _inproj_kernel,
        grid=(n // bm,),
        in_specs=[
            pl.BlockSpec((bm, D_MODEL), lambda i: (i, 0)),
            pl.BlockSpec((None, 1, D_MODEL), lambda i: (layer, 0, 0)),
            pl.BlockSpec((None, D_MODEL, IN_WP), lambda i: (layer, 0, 0)),
        ],
        out_specs=pl.BlockSpec((bm, IN_WP), lambda i: (i, 0)),
        out_shape=jax.ShapeDtypeStruct((n, IN_WP), F32),
        compiler_params=_params(("parallel",)),
        name="inproj",
    )(x, norm_g, w_in_bf)


def _outproj_kernel(x_ref, a_ref, b_ref, c_ref, w_ref, *rest, final):
    y = x_ref[...]
    y = y + _dot(a_ref[...].astype(BF16), w_ref[0:SBW, :])
    y = y + _dot(b_ref[...].astype(BF16), w_ref[SBW:SBW + CCH, :])
    y = y + _dot(c_ref[...].astype(BF16), w_ref[SBW + CCH:, :])
    if final:
        g_ref, y_ref = rest
        ms = jnp.mean(y * y, axis=-1, keepdims=True)
        y_ref[...] = y * lax.rsqrt(ms + EPS) * g_ref[...]
    else:
        (y_ref,) = rest
        y_ref[...] = y


def _outproj(x, mix_sb, mix_cv, mix_gla, w_out_bf, layer, bm, final_g=None):
    n = x.shape[0]
    final = final_g is not None
    in_specs = [
        pl.BlockSpec((bm, D_MODEL), lambda i: (i, 0)),
        pl.BlockSpec((bm, SBW), lambda i: (i, 0)),
        pl.BlockSpec((bm, CCH), lambda i: (i, 0)),
        pl.BlockSpec((bm, GVW), lambda i: (i, 0)),
        pl.BlockSpec((None, D_MODEL, D_MODEL), lambda i: (layer, 0, 0)),
    ]
    args = [x, mix_sb, mix_cv, mix_gla, w_out_bf]
    if final:
        in_specs.append(pl.BlockSpec((1, D_MODEL), lambda i: (0, 0)))
        args.append(final_g)
    return pl.pallas_call(
        functools.partial(_outproj_kernel, final=final),
        grid=(n // bm,),
        in_specs=in_specs,
        out_specs=pl.BlockSpec((bm, D_MODEL), lambda i: (i, 0)),
        out_shape=jax.ShapeDtypeStruct((n, D_MODEL), F32),
        compiler_params=_params(("parallel",)),
        name="outproj",
    )(*args)


def _neg_abs(x):
    bits = lax.bitcast_convert_type(x, jnp.uint32) | jnp.uint32(0x80000000)
    return lax.bitcast_convert_type(bits, F32)


def _sb_logs(z):
    l = jnp.log(1.0 + jnp.exp2(_neg_abs(z))) * LOG2E
    ls = jnp.minimum(z, 0.0) - l
    return ls, ls - z


def _lane_tile(x, n):
    reps = n // x.shape[1]
    return x if reps == 1 else jnp.concatenate([x] * reps, axis=1)


def _sb_weights(ls, cr, carry, mask):
    n = ls.shape[1]
    w = jnp.exp2(ls + cr[:, :n] + _lane_tile(carry, n))
    if mask is not None:
        w = jnp.where(mask, w, 0.0)
    return w.astype(BF16)


def _sb_block(qs, kb, vb, uo, carry_ref, acc_ref, mask, r0=0):
    ls, lf = _sb_logs(_dot(qs, kb))
    if mask is not None:
        lf = jnp.where(mask, lf, 0.0)
    lfb = lf.astype(BF16)
    cr = _dot(lfb, uo)
    acc_ref[r0:, :] += _dot_nt(_sb_weights(ls, cr, carry_ref[r0:, :], mask), vb)
    tot = cr[:, 0:1] + lfb[:, 0:1].astype(F32)
    carry_ref[r0:, :] += jnp.broadcast_to(tot, (tot.shape[0], LANE))


def _sb_blocks_staged(qs, kb, vb, bias, uo, carry_ref, acc_ref, mask, transposed):
    m = qs[0].shape[0]
    n = uo.shape[0]
    logs = [_sb_logs((_dot(q, k) if transposed else _dot_nt(q, k)) + bias) for q, k in zip(qs, kb)]
    lfs = [(lf if mask is None else jnp.where(mask, lf, 0.0)).astype(BF16) for _, lf in logs]
    cr_all = _dot(jnp.concatenate(lfs, axis=0), uo)
    for g in range(len(qs)):
        cr = cr_all[g * m:(g + 1) * m]
        w = _sb_weights(logs[g][0], cr, carry_ref[g], mask)
        acc_ref[g] += _dot_nt(w, vb[g]) if transposed else _dot(w, vb[g])
        carry_ref[g] += cr[:, n:]


class _SampleUnit:
    def __init__(self, j, n_units, n_pages, s, q_ref, kn_ref, vn_ref, gate_ref, kp_refs, vp_refs, bias_ref, uo_ref,
                 o_ref, qs_ref, acc_ref, carry_ref):
        self.group = len(kp_refs)
        self.s, self.n_pages = s, n_pages
        self.m = NH * s
        self.page = uo_ref.shape[0]
        self.gi = j // (n_pages + 1)
        self.ps = j - self.gi * (n_pages + 1)
        self.live = j < n_units
        slot = self.gi & 1
        self.q_ref, self.kn_ref, self.vn_ref, self.gate_ref = q_ref, kn_ref, vn_ref, gate_ref
        self.kp_refs, self.vp_refs, self.bias_ref, self.uo_ref, self.o_ref = kp_refs, vp_refs, bias_ref, uo_ref, o_ref
        self.qs_ref, self.acc_ref, self.carry_ref = qs_ref.at[slot], acc_ref.at[slot], carry_ref.at[slot]

    def seq_rows(self, g):
        return pl.ds(pl.multiple_of((self.gi * self.group + g) * self.s, self.s), self.s)

    def start(self):
        s, m, page, group = self.s, self.m, self.page, self.group

        @pl.when(jnp.logical_and(self.live, self.ps == 0))
        def _():
            head = lax.broadcasted_iota(jnp.int32, (s, SBW), 1) // DH
            row = lax.rem(lax.broadcasted_iota(jnp.int32, (m, page), 0), s)
            col = lax.broadcasted_iota(jnp.int32, (m, page), 1)
            pad = jnp.zeros((page - s, SBW), F32)
            self.acc_ref[...] = jnp.zeros_like(self.acc_ref)
            self.carry_ref[...] = jnp.zeros_like(self.carry_ref)
            qs, kb, vb = [], [], []
            for g in range(group):
                q = self.q_ref[self.seq_rows(g), :] * Q_SCALE
                qs.append(jnp.concatenate([jnp.where(head == h, q, 0.0) for h in range(NH)], axis=0).astype(BF16))
                self.qs_ref[g] = qs[g]
                kb.append(jnp.concatenate([self.kn_ref[self.seq_rows(g), :], pad], axis=0).astype(BF16))
                vb.append(jnp.concatenate([self.vn_ref[self.seq_rows(g), :], pad], axis=0).astype(BF16))
            _sb_blocks_staged(qs, kb, vb, self.bias_ref[...], self.uo_ref[...], self.carry_ref, self.acc_ref,
                              col < row, False)

    def page_step(self):
        group = self.group
        on = jnp.where(jnp.logical_and(self.live, self.ps > 0), 1, 0)
        mask = jnp.broadcast_to(on, (self.m, self.page)) > 0
        _sb_blocks_staged([self.qs_ref[g] for g in range(group)],
                          [self.kp_refs[g][...].astype(BF16) for g in range(group)],
                          [self.vp_refs[g][...].astype(BF16) for g in range(group)],
                          self.bias_ref[...], self.uo_ref[...], self.carry_ref, self.acc_ref, mask, True)

    def finish(self):
        s = self.s

        @pl.when(jnp.logical_and(self.live, self.ps == self.n_pages))
        def _():
            head = lax.broadcasted_iota(jnp.int32, (s, SBW), 1) // DH
            for g in range(self.group):
                o = jnp.zeros((s, SBW), F32)
                for h in range(NH):
                    o = jnp.where(head == h, self.acc_ref[g, h * s:(h + 1) * s, :], o)
                self.o_ref[self.seq_rows(g), :] = o * _silu(self.gate_ref[self.seq_rows(g), :])


def _attn_kernel(pt_ref, q_ref, k_ref, v_ref, gate_ref, bias_ref, uo_ref, sq_ref, skn_ref, svn_ref, sgate_ref, *rest,
                 nsub, n_pages, group, units, n_units, s):
    del pt_ref
    npg = units * group
    kp_refs, vp_refs = rest[:npg], rest[npg:2 * npg]
    sbias_ref, suo_ref, o_ref, so_ref, qs_ref, acc_ref, carry_ref, sqs_ref, sacc_ref, scarry_ref = rest[2 * npg:]
    step = (pl.program_id(0) * pl.num_programs(1) + pl.program_id(1)) * pl.num_programs(2) + pl.program_id(2)
    sample_units = [
        _SampleUnit(step * units + un, n_units, n_pages, s, sq_ref, skn_ref, svn_ref, sgate_ref,
                    kp_refs[un * group:(un + 1) * group], vp_refs[un * group:(un + 1) * group], sbias_ref, suo_ref,
                    so_ref, sqs_ref, sacc_ref, scarry_ref)
        for un in range(units)]
    for unit in sample_units:
        unit.start()
    for unit in sample_units:
        unit.page_step()

    qi = pl.program_id(2)
    qb = nsub * BLK
    lane = lax.broadcasted_iota(jnp.int32, (BLK, LANE), 1)
    for g in range(nsub):
        q = q_ref[g * BLK:(g + 1) * BLK, :]
        zero = jnp.zeros_like(q)
        qs_ref[2 * g * BLK:(2 * g + 1) * BLK, :LANE] = jnp.where(lane < DH, q, zero)
        qs_ref[(2 * g + 1) * BLK:(2 * g + 2) * BLK, :LANE] = jnp.where(lane >= DH, q, zero)
    qs_ref[:, LANE:] = bias_ref[...]
    ones_rows = jnp.where(lax.broadcasted_iota(jnp.int32, (LANE, KBLK), 0) < N_BIAS_PIECES, 1.0, 0.0).astype(BF16)
    acc_ref[...] = jnp.zeros_like(acc_ref)
    carry_ref[...] = jnp.zeros_like(carry_ref)
    uo = uo_ref[...]
    for c in reversed(range(qb // KBLK)):
        r0 = 2 * c * KBLK
        m = 2 * qb - r0
        row = lax.broadcasted_iota(jnp.int32, (m, KBLK), 0)
        col = lax.broadcasted_iota(jnp.int32, (m, KBLK), 1)
        first_hidden = (row // (2 * BLK)) * BLK + (row & (BLK - 1))
        mask = col < jnp.where(row >= 2 * KBLK, KBLK, first_hidden)
        ks = pl.multiple_of(qi * qb + c * KBLK, KBLK)
        kb = jnp.concatenate([k_ref[:, pl.ds(ks, KBLK)], ones_rows], axis=0)
        _sb_block(qs_ref[r0:, :], kb, v_ref[:, pl.ds(ks, KBLK)], uo, carry_ref, acc_ref, mask, r0=r0)

    for unit in sample_units:
        unit.finish()

    def older_span(kbase):
        for c in reversed(range(qb // KBLK)):
            ks = pl.multiple_of(kbase + c * KBLK, KBLK)
            kb = jnp.concatenate([k_ref[:, pl.ds(ks, KBLK)], ones_rows], axis=0)
            _sb_block(qs_ref[...], kb, v_ref[:, pl.ds(ks, KBLK)], uo, carry_ref, acc_ref, None)

    trips = qi // SPANS_PER_TRIP

    def body(t, carry):
        kbase = (qi - 1 - SPANS_PER_TRIP * t) * qb
        for i in range(SPANS_PER_TRIP):
            older_span(kbase - i * qb)
        return carry

    lax.fori_loop(0, trips, body, 0)
    left = qi - SPANS_PER_TRIP * trips
    for r in range(1, SPANS_PER_TRIP):
        @pl.when(left == r)
        def _(r=r):
            for i in reversed(range(r)):
                older_span(i * qb)

    for g in range(nsub):
        o = jnp.where(lane < DH, acc_ref[2 * g * BLK:(2 * g + 1) * BLK, :],
                      acc_ref[(2 * g + 1) * BLK:(2 * g + 2) * BLK, :])
        o_ref[g * BLK:(g + 1) * BLK, :] = (o * _silu(gate_ref[g * BLK:(g + 1) * BLK, :])).astype(o_ref.dtype)


def _attn(q, kvt, u, ubase, bias_pairs, uo, nseq, tpad, nsub, u_s, cache_k, cache_v, page_table, bias_rows, uo_s,
          layer, nseq_s, s, group):
    n = nseq * tpad
    qb = nsub * BLK
    nq = tpad // qb
    hp = NH // 2
    gcol = (OFF_SBG - ubase) // LANE
    n_pages = page_table.shape[1]
    page = cache_k.shape[3]
    n_units = (nseq_s // group) * (n_pages + 1)
    n_steps = nseq * hp * nq
    units = -(-n_units // n_steps)
    ns = nseq_s * s

    j = jnp.minimum(jnp.arange(n_steps)[:, None, None] * units + jnp.arange(units)[None, :, None], n_units - 1)
    gi = j // (n_pages + 1)
    ps = j - gi * (n_pages + 1)
    step_pages = page_table[gi * group + jnp.arange(group)[None, None, :], n_pages - jnp.maximum(ps, 1)]
    step_pages = step_pages.reshape(n_steps, units * group)

    def page_map(un, g):
        return lambda b, h, i, pt: (layer, pt[(b * hp + h) * nq + i, un * group + g], 0, 0)

    page_specs = [pl.BlockSpec((None, None, SBW, page), page_map(un, g)) for un in range(units) for g in range(group)]

    def sample_cols(off):
        return pl.BlockSpec((ns, SBW), lambda b, h, i, pt: (0, off // SBW))

    grid_spec = pltpu.PrefetchScalarGridSpec(
        num_scalar_prefetch=1,
        grid=(nseq, hp, nq),
        in_specs=[
            pl.BlockSpec((qb, LANE), lambda b, h, i, pt: (b * nq + i, h)),
            pl.BlockSpec((None, LANE, tpad), lambda b, h, i, pt: (b, h, 0)),
            pl.BlockSpec((None, LANE, tpad), lambda b, h, i, pt: (b, hp + h, 0)),
            pl.BlockSpec((qb, LANE), lambda b, h, i, pt: (b * nq + i, gcol + h)),
            pl.BlockSpec((None, 2 * qb, LANE), lambda b, h, i, pt: (h, 0, 0)),
            pl.BlockSpec((KBLK, KBLK), lambda b, h, i, pt: (0, 0)),
            sample_cols(OFF_Q), sample_cols(OFF_K), sample_cols(OFF_V), sample_cols(OFF_SBG),
            *page_specs, *page_specs,
            pl.BlockSpec((NH * s, page), lambda b, h, i, pt: (0, 0)),
            pl.BlockSpec((page, 2 * page), lambda b, h, i, pt: (0, 0)),
        ],
        out_specs=[
            pl.BlockSpec((qb, LANE), lambda b, h, i, pt: (b * nq + i, h)),
            pl.BlockSpec((ns, SBW), lambda b, h, i, pt: (0, 0)),
        ],
        scratch_shapes=[pltpu.VMEM((2 * qb, 2 * LANE), BF16), pltpu.VMEM((2 * qb, LANE), F32),
                        pltpu.VMEM((2 * qb, LANE), F32),
                        pltpu.VMEM((2, group, NH * s, SBW), BF16), pltpu.VMEM((2, group, NH * s, SBW), F32),
                        pltpu.VMEM((2, group, NH * s, page), F32)],
    )
    return pl.pallas_call(
        functools.partial(_attn_kernel, nsub=nsub, n_pages=n_pages, group=group, units=units, n_units=n_units, s=s),
        grid_spec=grid_spec,
        out_shape=[jax.ShapeDtypeStruct((n, SBW), BF16), jax.ShapeDtypeStruct((ns, SBW), F32)],
        compiler_params=pltpu.CompilerParams(dimension_semantics=("arbitrary", "arbitrary", "arbitrary"),
                                             vmem_limit_bytes=ATTN_VMEM_LIMIT),
        name="attn",
    )(step_pages, q, kvt, kvt, u, bias_pairs, uo, u_s, u_s, u_s, u_s,
      *([cache_k] * (units * group)), *([cache_v] * (units * group)), bias_rows, uo_s)


def _conv_kernel(cin_ref, cg_ref, st_ref, w_ref, b_ref, lng_ref, lnb_ref, pw_ref, o_ref, stout_ref,
                 full_ref, shift_ref, y_ref, *, tb, sub, last_t, state_off):
    t = pl.program_id(1)

    @pl.when(t == 0)
    def _():
        full_ref[0:CPAD, :] = st_ref[...]

    cin = cin_ref[...]
    full_ref[CPAD:CPAD + tb, :] = cin[:, :CCH] * jax.nn.sigmoid(cin[:, CCH:])
    nshift = shift_ref.shape[1]
    for s in range(1, 8):
        shift_ref[s - 1] = full_ref[s:s + nshift, :]
    lead = CPAD - (CW - 1)
    for r0 in range(0, tb, sub):
        acc = jnp.zeros((sub, CCH), F32) + b_ref[...]
        for j in range(CW):
            a, s = divmod(lead + j, 8)
            lo = r0 + 8 * a
            rows = full_ref[lo:lo + sub, :] if s == 0 else shift_ref[s - 1, lo:lo + sub, :]
            acc = acc + jnp.concatenate([w_ref[8 * j:8 * j + 8, :]] * (sub // 8), axis=0) * rows
        y_ref[r0:r0 + sub, :] = acc
    c = y_ref[...]
    mu = jnp.mean(c, axis=-1, keepdims=True)
    d = c - mu
    var = jnp.mean(d * d, axis=-1, keepdims=True)
    c = d * lax.rsqrt(var + EPS) * lng_ref[...] + lnb_ref[...]
    c = _dot(_silu(c).astype(BF16), pw_ref[...])
    o_ref[...] = (c * _silu(cg_ref[...])).astype(o_ref.dtype)

    @pl.when(t == last_t)
    def _():
        stout_ref[...] = full_ref[state_off:state_off + CPAD, :]

    full_ref[0:CPAD, :] = full_ref[tb:tb + CPAD, :]


def _conv(u, ubase, state_pad, dw_w, dw_b, ln_g, ln_b, pw2_bf, layer, nseq, tpad, t_real, tb, out_dtype):
    nt = tpad // tb
    last_t = (t_real - 1) // tb
    state_off = t_real - last_t * tb
    sub = min(tb, 32)
    kern = functools.partial(_conv_kernel, tb=tb, sub=sub, last_t=last_t, state_off=state_off)
    vec = pl.BlockSpec((None, 1, CCH), lambda b, t: (layer, 0, 0))
    return pl.pallas_call(
        kern,
        grid=(nseq, nt),
        in_specs=[
            pl.BlockSpec((tb, 2 * CCH), lambda b, t: (b * nt + t, (OFF_CIN - ubase) // (2 * CCH))),
            pl.BlockSpec((tb, CCH), lambda b, t: (b * nt + t, (OFF_CG - ubase) // CCH)),
            pl.BlockSpec((None, CPAD, CCH), lambda b, t: (b, 0, 0)),
            pl.BlockSpec((None, 8 * CW, CCH), lambda b, t: (layer, 0, 0)),
            vec, vec, vec,
            pl.BlockSpec((None, CCH, CCH), lambda b, t: (layer, 0, 0)),
        ],
        out_specs=[
            pl.BlockSpec((tb, CCH), lambda b, t: (b * nt + t, 0)),
            pl.BlockSpec((None, CPAD, CCH), lambda b, t: (b, 0, 0)),
        ],
        out_shape=[
            jax.ShapeDtypeStruct((nseq * tpad, CCH), out_dtype),
            jax.ShapeDtypeStruct((nseq, CPAD, CCH), F32),
        ],
        scratch_shapes=[pltpu.VMEM((CPAD + tb, CCH), F32), pltpu.VMEM((7, tb + CPAD - 8, CCH), F32),
                        pltpu.VMEM((tb, CCH), F32)],
        compiler_params=_params(("arbitrary", "arbitrary")),
        name="conv",
    )(u, u, state_pad, dw_w, dw_b, ln_g, ln_b, pw2_bf)


def _gla_kernel(gq_ref, gk_ref, gv_ref, gg_ref, glr_ref, s0_ref, w2h_ref, w2l_ref, gkb_ref, ng_ref,
                ts_ref, gmat_ref, o_ref, sout_ref, s_ref, *, rows, t_real, n_chunks):
    c = pl.program_id(1)
    seqs = range(gq_ref.shape[0])

    @pl.when(c == 0)
    def _():
        s_ref[...] = s0_ref[...]

    def padrows(x):
        if rows == BLK:
            return x
        return jnp.concatenate([x, jnp.zeros((BLK - rows, x.shape[1]), x.dtype)], axis=0)

    def rows_of(x, g):
        return x[g * BLK:(g + 1) * BLK]

    ridx = lax.broadcasted_iota(jnp.int32, (BLK, GKW), 0)
    valid = ridx < jnp.minimum(rows, t_real - c * rows)
    valid_all = jnp.concatenate([valid] * len(seqs), axis=0)
    gq = [padrows(gq_ref[g]) * (GDK ** -0.5) for g in seqs]
    gk = [jnp.where(valid, padrows(gk_ref[g]), 0.0) for g in seqs]
    gv_bf = [padrows(gv_ref[g]).astype(BF16) for g in seqs]
    glr_h, glr_l = _split2(jnp.concatenate([padrows(glr_ref[g]) for g in seqs], axis=0))
    x = _dot(glr_h, w2h_ref[...]) + _dot(glr_l, w2h_ref[...]) + _dot(glr_h, w2l_ref[...]) + gkb_ref[...]
    lg_all = jnp.where(valid_all, _log_sigmoid(x) * (1.0 / GATE_NORM), 0.0)
    lg = [rows_of(lg_all, g) for g in seqs]
    ts = ts_ref[...]
    lg_h, lg_m, lg_l = _split3(jnp.concatenate(lg, axis=1))
    br = _dot(ts, lg_h) + _dot(ts, lg_m) + _dot(ts, lg_l)
    b = [br[:BLK, g * GKW:(g + 1) * GKW] for g in seqs]
    r = [br[BLK:, g * GKW:(g + 1) * GKW] for g in seqs]
    s_bd = [s_ref[g] for g in seqs]
    o_inter = [_dot((gq[g] * jnp.exp(b[g])).astype(BF16), s_bd[g].astype(BF16)) for g in seqs]
    qd = [gq[g] * jnp.exp(b[g] - r[g]) for g in seqs]
    nst = GH * GSUB
    qmask = (lax.broadcasted_iota(jnp.int32, (nst, GKW), 1) // GDK) == (lax.broadcasted_iota(jnp.int32, (nst, GKW), 0) // GSUB)
    vmask = (lax.broadcasted_iota(jnp.int32, (nst, GVW), 1) // GDV) == (lax.broadcasted_iota(jnp.int32, (nst, GVW), 0) // GSUB)
    trow = lax.rem(lax.broadcasted_iota(jnp.int32, (nst, BLK), 0), GSUB)
    scol = lax.broadcasted_iota(jnp.int32, (nst, BLK), 1)
    nsc = BLK // GSUB
    att = [[] for _ in seqs]
    for i in range(nsc):
        lo = i * GSUB
        kd = [(gk[g] * jnp.exp(r[g][lo:lo + 1, :] - b[g])).astype(BF16) for g in seqs]
        qst = [jnp.where(qmask, jnp.concatenate([qd[g][lo:lo + GSUB]] * GH, axis=0), 0.0).astype(BF16) for g in seqs]
        for g in seqs:
            att[g].append(jnp.where(scol <= trow + lo, _dot_nt(qst[g], kd[g]), 0.0).astype(BF16))
    vmask_all = jnp.concatenate([vmask] * nsc, axis=0)
    ov = [jnp.where(vmask_all, _dot(jnp.concatenate(att[g], axis=0), gv_bf[g]), 0.0) for g in seqs]

    def heads_summed(x, i):
        parts = [x[(i * GH + h) * GSUB:(i * GH + h + 1) * GSUB] for h in range(GH)]
        return functools.reduce(lambda acc, part: acc + part, parts)

    o = jnp.concatenate([jnp.concatenate([heads_summed(ov[g], i) for i in range(nsc)], axis=0) + o_inter[g]
                         for g in seqs], axis=0)
    o2_h, o2_l = _split2(o * o)
    ms = _dot(o2_h, gmat_ref[...]) + _dot(o2_l, gmat_ref[...])
    gate = jnp.concatenate([padrows(gg_ref[g]) for g in seqs], axis=0)
    res = o * lax.rsqrt(ms + EPS) * ng_ref[...] * _silu(gate)
    for g in seqs:
        o_ref[g] = rows_of(res, g)[:rows].astype(o_ref.dtype)
    kdec_t = [(gk[g] * jnp.exp(b[g][BLK - 1:BLK, :] - b[g])).T.astype(BF16) for g in seqs]
    kv = [_dot(kdec_t[g], gv_bf[g]) for g in seqs]
    bdmask = (lax.broadcasted_iota(jnp.int32, (GKW, GVW), 0) // GDK) == (lax.broadcasted_iota(jnp.int32, (GKW, GVW), 1) // GDV)
    for g in seqs:
        bl_col = jnp.sum(lg[g].T, axis=1, keepdims=True)
        s_ref[g] = jnp.exp(bl_col) * s_bd[g] + jnp.where(bdmask, kv[g], 0.0)

    @pl.when(c == n_chunks - 1)
    def _():
        sout_ref[...] = s_ref[...]


def _gla(u, ubase, s0_bd, w2h, w2l, gkb, ng_tiled, ts, gmat, layer, nseq, tpad, t_real, rows, group, out_dtype):
    nc = tpad // rows
    u3 = u.reshape(nseq, tpad, u.shape[1])
    kern = functools.partial(_gla_kernel, rows=rows, t_real=t_real, n_chunks=nc)

    def cols(off, width):
        return pl.BlockSpec((group, rows, width), lambda b, c: (b, c, (off - ubase) // width))

    state_spec = pl.BlockSpec((group, GKW, GVW), lambda b, c: (b, 0, 0))
    o, s_out = pl.pallas_call(
        kern,
        grid=(nseq // group, nc),
        in_specs=[
            cols(OFF_GQ, GKW), cols(OFF_GK, GKW), cols(OFF_GV, GVW), cols(OFF_GG, GVW), cols(OFF_GLR, LANE),
            state_spec,
            pl.BlockSpec((None, LANE, GKW), lambda b, c: (layer, 0, 0)),
            pl.BlockSpec((None, LANE, GKW), lambda b, c: (layer, 0, 0)),
            pl.BlockSpec((None, 1, GKW), lambda b, c: (layer, 0, 0)),
            pl.BlockSpec((None, 1, GVW), lambda b, c: (layer, 0, 0)),
            pl.BlockSpec((2 * BLK, BLK), lambda b, c: (0, 0)),
            pl.BlockSpec((GVW, GVW), lambda b, c: (0, 0)),
        ],
        out_specs=[
            pl.BlockSpec((group, rows, GVW), lambda b, c: (b, c, 0)),
            state_spec,
        ],
        out_shape=[
            jax.ShapeDtypeStruct((nseq, tpad, GVW), out_dtype),
            jax.ShapeDtypeStruct((nseq, GKW, GVW), F32),
        ],
        scratch_shapes=[pltpu.VMEM((group, GKW, GVW), F32)],
        compiler_params=_params(("arbitrary", "arbitrary")),
        name="gla",
    )(u3, u3, u3, u3, u3, s0_bd, w2h, w2l, gkb, ng_tiled, ts, gmat)
    return o.reshape(nseq * tpad, GVW), s_out


def _state_to_blockdiag(s):
    n = s.shape[0]
    out = jnp.zeros((n, GH, GDK, GH, GDV), F32)
    for h in range(GH):
        out = out.at[:, h, :, h, :].set(s[:, h].astype(F32))
    return out.reshape(n, GKW, GVW)


def _blockdiag_to_state(s_bd):
    n = s_bd.shape[0]
    s5 = s_bd.reshape(n, GH, GDK, GH, GDV)
    return jnp.stack([s5[:, h, :, h, :] for h in range(GH)], axis=1)


def _row_block(n):
    for bm in (512, 256, 128, 64, 32, 16, 8):
        if n % bm == 0:
            return bm
    raise ValueError(f"row count {n} is not a multiple of 8")


def kernel(x_prompt, x_sample, cache_k, cache_v, state_conv, state_gla, page_table, meta_tokens, norm_g, w_in,
           sb_bias, conv_dw_w, conv_dw_b, conv_ln_g, conv_ln_b, conv_pw2, gla_gk_w2, gla_gk_b, gla_norm_g, w_out,
           final_norm_g):
    bp, seq, _ = x_prompt.shape
    bd, dseq, _ = x_sample.shape
    depth = w_in.shape[0]
    n_pool, page = cache_k.shape[1], cache_k.shape[2]
    assert page == BLK and dseq % 8 == 0 and BLK % dseq == 0
    t_real = N_META + seq
    tpad = -(-t_real // KBLK) * KBLK

    w_in_bf = jnp.pad(w_in, ((0, 0), (0, 0), (0, IN_WP - IN_W))).astype(BF16)
    wkv_t = jnp.transpose(w_in[:, :, OFF_K:OFF_SBG], (0, 2, 1)).astype(BF16)
    w_out_bf = w_out.astype(BF16)
    norm_g3 = norm_g.reshape(depth, 1, D_MODEL)
    dw_w = jnp.repeat(conv_dw_w, 8, axis=1)
    dw_b = conv_dw_b.reshape(depth, 1, CCH)
    ln_g = conv_ln_g.reshape(depth, 1, CCH)
    ln_b = conv_ln_b.reshape(depth, 1, CCH)
    pw2_bf = conv_pw2.astype(BF16)
    w2 = jnp.pad(gla_gk_w2, ((0, 0), (0, LANE - GRANK), (0, 0)))
    w2h = w2.astype(BF16)
    w2l = (w2 - w2h.astype(F32)).astype(BF16)
    gkb = gla_gk_b.reshape(depth, 1, GKW)
    ng_tiled = jnp.tile(gla_norm_g, (1, GH)).reshape(depth, 1, GVW)
    ii = jnp.arange(BLK)
    uo = jnp.concatenate([(ii[:, None] > ii[None, :]), jnp.ones((BLK, LANE), bool)], axis=1).astype(BF16)
    ik = jnp.arange(KBLK)
    uo_p = (ik[:, None] > ik[None, :]).astype(BF16)
    tri = ii[None, :] <= ii[:, None]
    sel = ii[None, :] < (ii[:, None] // GSUB) * GSUB
    ts = jnp.concatenate([tri, sel], axis=0).astype(BF16)
    gi = jnp.arange(GVW) // GDV
    gmat = ((gi[:, None] == gi[None, :]).astype(F32) / GDV).astype(BF16)
    final_g = final_norm_g.reshape(1, D_MODEL)

    meta = jnp.broadcast_to(meta_tokens[None].astype(F32), (bp, N_META, D_MODEL))
    xp = jnp.concatenate([meta, x_prompt, jnp.zeros((bp, tpad - t_real, D_MODEL), F32)], axis=1)
    xp = xp.reshape(bp * tpad, D_MODEL)
    xs = x_sample.reshape(bd * dseq, D_MODEL)
    bm_p = _row_block(bp * tpad)
    bm_s = _row_block(bd * dseq)
    conv_tb = KBLK if tpad % KBLK == 0 else BLK
    nsub = max(d for d in (2, 4, 6) if (tpad // BLK) % d == 0)
    group = max(d for d in range(1, 9) if bd % d == 0)
    gla_group_p = max(d for d in range(1, 5) if bp % d == 0)
    gla_group_s = max(d for d in range(1, 5) if bd % d == 0)
    cache_k4 = jnp.transpose(cache_k, (0, 1, 3, 4, 2)).reshape(depth, n_pool, SBW, page)
    cache_v4 = jnp.transpose(cache_v, (0, 1, 3, 4, 2)).reshape(depth, n_pool, SBW, page)
    zero_conv = jnp.zeros((bp, CPAD, CCH), F32)
    zero_gla = jnp.zeros((bp, GKW, GVW), F32)

    cp_l, sp_l, ks_l, vs_l, cs_l, ss_l = ([] for _ in range(6))
    kt_all = vt_all = None
    for l in range(depth):
        bias2 = sb_bias[l] * LOG2E
        bias_rows_p = jnp.tile(jnp.repeat(bias2.reshape(NH // 2, 2), BLK, axis=1), (1, nsub))
        bias_pairs = jnp.pad(jnp.stack(_split3(bias_rows_p), axis=-1), ((0, 0), (0, 0), (0, LANE - N_BIAS_PIECES)))
        bias_rows = jnp.broadcast_to(jnp.repeat(bias2, dseq)[:, None], (NH * dseq, page))
        last = l == depth - 1
        u_p, q_p, kt_all, vt_all, kvt_p = _inproj_prompt(xp, norm_g3, w_in_bf, wkv_t, kt_all, vt_all, l, bp, tpad,
                                                         t_real, depth, KBLK)
        u_s = _inproj(xs, norm_g3, w_in_bf, l, bm_s)
        mix_sb, smix_sb = _attn(q_p, kvt_p, u_p, OFF_SBG, bias_pairs, uo_p, bp, tpad, nsub, u_s, cache_k4, cache_v4,
                                page_table, bias_rows, uo, l, bd, dseq, group)
        mix_cv, cst_p = _conv(u_p, OFF_SBG, zero_conv, dw_w, dw_b, ln_g, ln_b, pw2_bf, l, bp, tpad, t_real, conv_tb,
                              BF16)
        mix_gla, gst_p = _gla(u_p, OFF_SBG, zero_gla, w2h, w2l, gkb, ng_tiled, ts, gmat, l, bp, tpad, t_real, BLK,
                              gla_group_p, BF16)
        xp = _outproj(xp, mix_sb, mix_cv, mix_gla, w_out_bf, l, bm_p, final_g if last else None)
        cp_l.append(cst_p[:, CPAD - (CW - 1):])
        sp_l.append(_blockdiag_to_state(gst_p))
        st_s = jnp.pad(state_conv[l], ((0, 0), (CPAD - (CW - 1), 0), (0, 0)))
        smix_cv, cst_s = _conv(u_s, 0, st_s, dw_w, dw_b, ln_g, ln_b, pw2_bf, l, bd, dseq, dseq, dseq, F32)
        smix_gla, gst_s = _gla(u_s, 0, _state_to_blockdiag(state_gla[l]), w2h, w2l, gkb, ng_tiled, ts, gmat, l, bd,
                               dseq, dseq, dseq, gla_group_s, F32)
        xs = _outproj(xs, smix_sb, smix_cv, smix_gla, w_out_bf, l, bm_s, final_g if last else None)
        ks_l.append(u_s[:, OFF_K:OFF_V].reshape(bd, dseq, NH, DH))
        vs_l.append(u_s[:, OFF_V:OFF_SBG].reshape(bd, dseq, NH, DH))
        cs_l.append(cst_s[:, CPAD - (CW - 1):])
        ss_l.append(_blockdiag_to_state(gst_s))

    y_prompt = xp.reshape(bp, tpad, D_MODEL)[:, N_META:t_real]
    y_sample = xs.reshape(bd, dseq, D_MODEL)
    new_k_prompt = jnp.transpose(kt_all[..., :t_real].reshape(depth, bp, NH, DH, t_real), (0, 1, 4, 2, 3))
    new_v_prompt = jnp.transpose(vt_all[..., :t_real].reshape(depth, bp, NH, DH, t_real), (0, 1, 4, 2, 3))
    return (y_prompt, y_sample, new_k_prompt, new_v_prompt, jnp.stack(cp_l), jnp.stack(sp_l),
            jnp.stack(ks_l), jnp.stack(vs_l), jnp.stack(cs_l), jnp.stack(ss_l))
```

```python
import functools

import jax
import jax.numpy as jnp
from jax import lax
from jax.experimental import pallas as pl
from jax.experimental.pallas import tpu as pltpu

F32 = jnp.float32
BF16 = jnp.bfloat16

D_MODEL = 1024
N_META = 16
NH = 8
DH = 64
SBW = NH * DH
CCH = 256
CW = 31
GH = 4
GDK = 32
GDV = 64
GKW = GH * GDK
GVW = GH * GDV
GRANK = 16
GATE_NORM = 16.0
EPS = 1e-6

LANE = 128
MXU_DIM = 256
BLK = 128
KBLK = MXU_DIM
SPANS_PER_TRIP = 3
LOG2E = 1.4426950408889634
Q_SCALE = DH ** -0.5 * LOG2E
N_BIAS_PIECES = 3
GSUB = 32
CPAD = 32
IN_W = 3600
IN_WP = 3712
OFF_Q, OFF_K, OFF_V, OFF_SBG = 0, 512, 1024, 1536
OFF_CIN, OFF_CG = 2048, 2560
OFF_GQ, OFF_GK, OFF_GV, OFF_GG, OFF_GLR = 2816, 2944, 3072, 3328, 3584
VMEM_LIMIT = 48 * 1024 * 1024
ATTN_VMEM_LIMIT = 56 * 1024 * 1024


def _dot(a, b):
    return jnp.dot(a, b, preferred_element_type=F32)


def _dot_nt(a, b):
    return lax.dot_general(a, b, (((1,), (1,)), ((), ())), preferred_element_type=F32)


def _split2(x):
    hi = x.astype(BF16)
    lo = (x - hi.astype(F32)).astype(BF16)
    return hi, lo


def _split3(x):
    hi = x.astype(BF16)
    r = x - hi.astype(F32)
    mid = r.astype(BF16)
    lo = (r - mid.astype(F32)).astype(BF16)
    return hi, mid, lo


def _log_sigmoid(x):
    return jnp.minimum(x, 0.0) - jnp.log(1.0 + jnp.exp(-jnp.abs(x)))


def _silu(x):
    return x * jax.nn.sigmoid(x)


def _params(sem):
    return pltpu.CompilerParams(dimension_semantics=sem, vmem_limit_bytes=VMEM_LIMIT)


def _inproj_kernel(x_ref, g_ref, w_ref, u_ref):
    x = x_ref[...]
    ms = jnp.mean(x * x, axis=-1, keepdims=True)
    h = (x * lax.rsqrt(ms + EPS) * g_ref[...]).astype(BF16)
    for c0 in range(0, IN_WP, SBW):
        wd = min(SBW, IN_WP - c0)
        u_ref[:, c0:c0 + wd] = _dot(h, w_ref[:, c0:c0 + wd])


def _inproj_prompt_kernel(x_ref, g_ref, w_ref, wkv_ref, *rest):
    u_ref, q_ref, kt_ref, vt_ref, kvbf_ref = rest[-5:]
    x = x_ref[...]
    ms = jnp.mean(x * x, axis=-1, keepdims=True)
    h = (x * lax.rsqrt(ms + EPS) * g_ref[...]).astype(BF16)
    q_ref[...] = (_dot(h, w_ref[:, OFF_Q:OFF_Q + SBW]) * Q_SCALE).astype(BF16)
    for c0 in range(OFF_SBG, IN_WP, SBW):
        wd = min(SBW, IN_WP - c0)
        u_ref[:, c0 - OFF_SBG:c0 - OFF_SBG + wd] = _dot(h, w_ref[:, c0:c0 + wd])
    kvt = _dot_nt(wkv_ref[...], h)
    kt_ref[...] = kvt[:SBW]
    vt_ref[...] = kvt[SBW:]
    kvbf_ref[...] = kvt.astype(BF16)


def _inproj_prompt(x, norm_g, w_in_bf, wkv_t, kt_prev, vt_prev, layer, nseq, tpad, t_real, depth, bm):
    n = x.shape[0]
    nt = tpad // bm
    in_specs = [
        pl.BlockSpec((bm, D_MODEL), lambda b, t: (b * nt + t, 0)),
        pl.BlockSpec((None, 1, D_MODEL), lambda b, t: (layer, 0, 0)),
        pl.BlockSpec((None, D_MODEL, IN_WP), lambda b, t: (layer, 0, 0)),
        pl.BlockSpec((None, 2 * SBW, D_MODEL), lambda b, t: (layer, 0, 0)),
    ]
    args = [x, norm_g, w_in_bf, wkv_t]
    aliases = {}
    if kt_prev is not None:
        in_specs += [pl.BlockSpec(memory_space=pl.ANY), pl.BlockSpec(memory_space=pl.ANY)]
        args += [kt_prev, vt_prev]
        aliases = {4: 2, 5: 3}
    cache = jax.ShapeDtypeStruct((depth, nseq, SBW, tpad), F32)
    cache_spec = pl.BlockSpec((None, None, SBW, bm), lambda b, t: (layer, b, 0, t))
    return pl.pallas_call(
        _inproj_prompt_kernel,
        grid=(nseq, nt),
        in_specs=in_specs,
        out_specs=[
            pl.BlockSpec((bm, IN_WP - OFF_SBG), lambda b, t: (b * nt + t, 0)),
            pl.BlockSpec((bm, SBW), lambda b, t: (b * nt + t, 0)),
            cache_spec, cache_spec,
            pl.BlockSpec((None, 2 * SBW, bm), lambda b, t: (b, 0, t)),
        ],
        out_shape=[
            jax.ShapeDtypeStruct((n, IN_WP - OFF_SBG), F32),
            jax.ShapeDtypeStruct((n, SBW), BF16),
            cache, cache,
            jax.ShapeDtypeStruct((nseq, 2 * SBW, tpad), BF16),
        ],
        input_output_aliases=aliases,
        compiler_params=_params(("parallel", "parallel")),
        name="inproj_prompt",
    )(*args)


def _inproj(x, norm_g, w_in_bf, layer, bm):
    n = x.shape[0]
    return pl.pallas_call(
        _inproj_kernel,
        grid=(n // bm,),
        in_specs=[
            pl.BlockSpec((bm, D_MODEL), lambda i: (i, 0)),
            pl.BlockSpec((None, 1, D_MODEL), lambda i: (layer, 0, 0)),
            pl.BlockSpec((None, D_MODEL, IN_WP), lambda i: (layer, 0, 0)),
        ],
        out_specs=pl.BlockSpec((bm, IN_WP), lambda i: (i, 0)),
        out_shape=jax.ShapeDtypeStruct((n, IN_WP), F32),
        compiler_params=_params(("parallel",)),
        name="inproj",
    )(x, norm_g, w_in_bf)


def _outproj_kernel(x_ref, a_ref, b_ref, c_ref, w_ref, *rest, final):
    y = x_ref[...]
    y = y + _dot(a_ref[...].astype(BF16), w_ref[0:SBW, :])
    y = y + _dot(b_ref[...].astype(BF16), w_ref[SBW:SBW + CCH, :])
    y = y + _dot(c_ref[...].astype(BF16), w_ref[SBW + CCH:, :])
    if final:
        g_ref, y_ref = rest
        ms = jnp.mean(y * y, axis=-1, keepdims=True)
        y_ref[...] = y * lax.rsqrt(ms + EPS) * g_ref[...]
    else:
        (y_ref,) = rest
        y_ref[...] = y


def _outproj(x, mix_sb, mix_cv, mix_gla, w_out_bf, layer, bm, final_g=None):
    n = x.shape[0]
    final = final_g is not None
    in_specs = [
        pl.BlockSpec((bm, D_MODEL), lambda i: (i, 0)),
        pl.BlockSpec((bm, SBW), lambda i: (i, 0)),
        pl.BlockSpec((bm, CCH), lambda i: (i, 0)),
        pl.BlockSpec((bm, GVW), lambda i: (i, 0)),
        pl.BlockSpec((None, D_MODEL, D_MODEL), lambda i: (layer, 0, 0)),
    ]
    args = [x, mix_sb, mix_cv, mix_gla, w_out_bf]
    if final:
        in_specs.append(pl.BlockSpec((1, D_MODEL), lambda i: (0, 0)))
        args.append(final_g)
    return pl.pallas_call(
        functools.partial(_outproj_kernel, final=final),
        grid=(n // bm,),
        in_specs=in_specs,
        out_specs=pl.BlockSpec((bm, D_MODEL), lambda i: (i, 0)),
        out_shape=jax.ShapeDtypeStruct((n, D_MODEL), F32),
        compiler_params=_params(("parallel",)),
        name="outproj",
    )(*args)


def _neg_abs(x):
    bits = lax.bitcast_convert_type(x, jnp.uint32) | jnp.uint32(0x80000000)
    return lax.bitcast_convert_type(bits, F32)


def _sb_logs(z):
    l = jnp.log(1.0 + jnp.exp2(_neg_abs(z))) * LOG2E
    ls = jnp.minimum(z, 0.0) - l
    return ls, ls - z


def _lane_tile(x, n):
    reps = n // x.shape[1]
    return x if reps == 1 else jnp.concatenate([x] * reps, axis=1)


def _sb_weights(ls, cr, carry, mask):
    n = ls.shape[1]
    w = jnp.exp2(ls + cr[:, :n] + _lane_tile(carry, n))
    if mask is not None:
        w = jnp.where(mask, w, 0.0)
    return w.astype(BF16)


def _sb_block(qs, kb, vb, uo, carry_ref, acc_ref, mask, r0=0):
    ls, lf = _sb_logs(_dot(qs, kb))
    if mask is not None:
        lf = jnp.where(mask, lf, 0.0)
    lfb = lf.astype(BF16)
    cr = _dot(lfb, uo)
    acc_ref[r0:, :] += _dot_nt(_sb_weights(ls, cr, carry_ref[r0:, :], mask), vb)
    tot = cr[:, 0:1] + lfb[:, 0:1].astype(F32)
    carry_ref[r0:, :] += jnp.broadcast_to(tot, (tot.shape[0], LANE))


def _sb_blocks_staged(qs, kb, vb, bias, uo, carry_ref, acc_ref, mask, transposed):
    m = qs[0].shape[0]
    n = uo.shape[0]
    logs = [_sb_logs((_dot(q, k) if transposed else _dot_nt(q, k)) + bias) for q, k in zip(qs, kb)]
    lfs = [(lf if mask is None else jnp.where(mask, lf, 0.0)).astype(BF16) for _, lf in logs]
    cr_all = _dot(jnp.concatenate(lfs, axis=0), uo)
    for g in range(len(qs)):
        cr = cr_all[g * m:(g + 1) * m]
        w = _sb_weights(logs[g][0], cr, carry_ref[g], mask)
        acc_ref[g] += _dot_nt(w, vb[g]) if transposed else _dot(w, vb[g])
        carry_ref[g] += cr[:, n:]


class _SampleUnit:
    def __init__(self, j, n_units, n_pages, s, q_ref, kn_ref, vn_ref, gate_ref, kp_refs, vp_refs, bias_ref, uo_ref,
                 o_ref, qs_ref, acc_ref, carry_ref):
        self.group = len(kp_refs)
        self.s, self.n_pages = s, n_pages
        self.m = NH * s
        self.page = uo_ref.shape[0]
        self.gi = j // (n_pages + 1)
        self.ps = j - self.gi * (n_pages + 1)
        self.live = j < n_units
        slot = self.gi & 1
        self.q_ref, self.kn_ref, self.vn_ref, self.gate_ref = q_ref, kn_ref, vn_ref, gate_ref
        self.kp_refs, self.vp_refs, self.bias_ref, self.uo_ref, self.o_ref = kp_refs, vp_refs, bias_ref, uo_ref, o_ref
        self.qs_ref, self.acc_ref, self.carry_ref = qs_ref.at[slot], acc_ref.at[slot], carry_ref.at[slot]

    def seq_rows(self, g):
        return pl.ds(pl.multiple_of((self.gi * self.group + g) * self.s, self.s), self.s)

    def start(self):
        s, m, page, group = self.s, self.m, self.page, self.group

        @pl.when(jnp.logical_and(self.live, self.ps == 0))
        def _():
            head = lax.broadcasted_iota(jnp.int32, (s, SBW), 1) // DH
            row = lax.rem(lax.broadcasted_iota(jnp.int32, (m, page), 0), s)
            col = lax.broadcasted_iota(jnp.int32, (m, page), 1)
            pad = jnp.zeros((page - s, SBW), F32)
            self.acc_ref[...] = jnp.zeros_like(self.acc_ref)
            self.carry_ref[...] = jnp.zeros_like(self.carry_ref)
            qs, kb, vb = [], [], []
            for g in range(group):
                q = self.q_ref[self.seq_rows(g), :] * Q_SCALE
                qs.append(jnp.concatenate([jnp.where(head == h, q, 0.0) for h in range(NH)], axis=0).astype(BF16))
                self.qs_ref[g] = qs[g]
                kb.append(jnp.concatenate([self.kn_ref[self.seq_rows(g), :], pad], axis=0).astype(BF16))
                vb.append(jnp.concatenate([self.vn_ref[self.seq_rows(g), :], pad], axis=0).astype(BF16))
            _sb_blocks_staged(qs, kb, vb, self.bias_ref[...], self.uo_ref[...], self.carry_ref, self.acc_ref,
                              col < row, False)

    def page_step(self):
        group = self.group
        on = jnp.where(jnp.logical_and(self.live, self.ps > 0), 1, 0)
        mask = jnp.broadcast_to(on, (self.m, self.page)) > 0
        _sb_blocks_staged([self.qs_ref[g] for g in range(group)],
                          [self.kp_refs[g][...].astype(BF16) for g in range(group)],
                          [self.vp_refs[g][...].astype(BF16) for g in range(group)],
                          self.bias_ref[...], self.uo_ref[...], self.carry_ref, self.acc_ref, mask, True)

    def finish(self):
        s = self.s

        @pl.when(jnp.logical_and(self.live, self.ps == self.n_pages))
        def _():
            head = lax.broadcasted_iota(jnp.int32, (s, SBW), 1) // DH
            for g in range(self.group):
                o = jnp.zeros((s, SBW), F32)
                for h in range(NH):
                    o = jnp.where(head == h, self.acc_ref[g, h * s:(h + 1) * s, :], o)
                self.o_ref[self.seq_rows(g), :] = o * _silu(self.gate_ref[self.seq_rows(g), :])


def _attn_kernel(pt_ref, q_ref, k_ref, v_ref, gate_ref, bias_ref, uo_ref, sq_ref, skn_ref, svn_ref, sgate_ref, *rest,
                 nsub, n_pages, group, units, n_units, s):
    del pt_ref
    npg = units * group
    kp_refs, vp_refs = rest[:npg], rest[npg:2 * npg]
    sbias_ref, suo_ref, o_ref, so_ref, qs_ref, acc_ref, carry_ref, sqs_ref, sacc_ref, scarry_ref = rest[2 * npg:]
    step = (pl.program_id(0) * pl.num_programs(1) + pl.program_id(1)) * pl.num_programs(2) + pl.program_id(2)
    sample_units = [
        _SampleUnit(step * units + un, n_units, n_pages, s, sq_ref, skn_ref, svn_ref, sgate_ref,
                    kp_refs[un * group:(un + 1) * group], vp_refs[un * group:(un + 1) * group], sbias_ref, suo_ref,
                    so_ref, sqs_ref, sacc_ref, scarry_ref)
        for un in range(units)]
    for unit in sample_units:
        unit.start()
    for unit in sample_units:
        unit.page_step()

    qi = pl.program_id(2)
    qb = nsub * BLK
    lane = lax.broadcasted_iota(jnp.int32, (BLK, LANE), 1)
    for g in range(nsub):
        q = q_ref[g * BLK:(g + 1) * BLK, :]
        zero = jnp.zeros_like(q)
        qs_ref[2 * g * BLK:(2 * g + 1) * BLK, :LANE] = jnp.where(lane < DH, q, zero)
        qs_ref[(2 * g + 1) * BLK:(2 * g + 2) * BLK, :LANE] = jnp.where(lane >= DH, q, zero)
    qs_ref[:, LANE:] = bias_ref[...]
    ones_rows = jnp.where(lax.broadcasted_iota(jnp.int32, (LANE, KBLK), 0) < N_BIAS_PIECES, 1.0, 0.0).astype(BF16)
    acc_ref[...] = jnp.zeros_like(acc_ref)
    carry_ref[...] = jnp.zeros_like(carry_ref)
    uo = uo_ref[...]
    for c in reversed(range(qb // KBLK)):
        r0 = 2 * c * KBLK
        m = 2 * qb - r0
        row = lax.broadcasted_iota(jnp.int32, (m, KBLK), 0)
        col = lax.broadcasted_iota(jnp.int32, (m, KBLK), 1)
        first_hidden = (row // (2 * BLK)) * BLK + (row & (BLK - 1))
        mask = col < jnp.where(row >= 2 * KBLK, KBLK, first_hidden)
        ks = pl.multiple_of(qi * qb + c * KBLK, KBLK)
        kb = jnp.concatenate([k_ref[:, pl.ds(ks, KBLK)], ones_rows], axis=0)
        _sb_block(qs_ref[r0:, :], kb, v_ref[:, pl.ds(ks, KBLK)], uo, carry_ref, acc_ref, mask, r0=r0)

    for unit in sample_units:
        unit.finish()

    def older_span(kbase):
        for c in reversed(range(qb // KBLK)):
            ks = pl.multiple_of(kbase + c * KBLK, KBLK)
            kb = jnp.concatenate([k_ref[:, pl.ds(ks, KBLK)], ones_rows], axis=0)
            _sb_block(qs_ref[...], kb, v_ref[:, pl.ds(ks, KBLK)], uo, carry_ref, acc_ref, None)

    trips = qi // SPANS_PER_TRIP

    def body(t, carry):
        kbase = (qi - 1 - SPANS_PER_TRIP * t) * qb
        for i in range(SPANS_PER_TRIP):
            older_span(kbase - i * qb)
        return carry

    lax.fori_loop(0, trips, body, 0)
    left = qi - SPANS_PER_TRIP * trips
    for r in range(1, SPANS_PER_TRIP):
        @pl.when(left == r)
        def _(r=r):
            for i in reversed(range(r)):
                older_span(i * qb)

    for g in range(nsub):
        o = jnp.where(lane < DH, acc_ref[2 * g * BLK:(2 * g + 1) * BLK, :],
                      acc_ref[(2 * g + 1) * BLK:(2 * g + 2) * BLK, :])
        o_ref[g * BLK:(g + 1) * BLK, :] = (o * _silu(gate_ref[g * BLK:(g + 1) * BLK, :])).astype(o_ref.dtype)


def _attn(q, kvt, u, ubase, bias_pairs, uo, nseq, tpad, nsub, u_s, cache_k, cache_v, page_table, bias_rows, uo_s,
          layer, nseq_s, s, group):
    n = nseq * tpad
    qb = nsub * BLK
    nq = tpad // qb
    hp = NH // 2
    gcol = (OFF_SBG - ubase) // LANE
    n_pages = page_table.shape[1]
    page = cache_k.shape[3]
    n_units = (nseq_s // group) * (n_pages + 1)
    n_steps = nseq * hp * nq
    units = -(-n_units // n_steps)
    ns = nseq_s * s

    j = jnp.minimum(jnp.arange(n_steps)[:, None, None] * units + jnp.arange(units)[None, :, None], n_units - 1)
    gi = j // (n_pages + 1)
    ps = j - gi * (n_pages + 1)
    step_pages = page_table[gi * group + jnp.arange(group)[None, None, :], n_pages - jnp.maximum(ps, 1)]
    step_pages = step_pages.reshape(n_steps, units * group)

    def page_map(un, g):
        return lambda b, h, i, pt: (layer, pt[(b * hp + h) * nq + i, un * group + g], 0, 0)

    page_specs = [pl.BlockSpec((None, None, SBW, page), page_map(un, g)) for un in range(units) for g in range(group)]

    def sample_cols(off):
        return pl.BlockSpec((ns, SBW), lambda b, h, i, pt: (0, off // SBW))

    grid_spec = pltpu.PrefetchScalarGridSpec(
        num_scalar_prefetch=1,
        grid=(nseq, hp, nq),
        in_specs=[
            pl.BlockSpec((qb, LANE), lambda b, h, i, pt: (b * nq + i, h)),
            pl.BlockSpec((None, LANE, tpad), lambda b, h, i, pt: (b, h, 0)),
            pl.BlockSpec((None, LANE, tpad), lambda b, h, i, pt: (b, hp + h, 0)),
            pl.BlockSpec((qb, LANE), lambda b, h, i, pt: (b * nq + i, gcol + h)),
            pl.BlockSpec((None, 2 * qb, LANE), lambda b, h, i, pt: (h, 0, 0)),
            pl.BlockSpec((KBLK, KBLK), lambda b, h, i, pt: (0, 0)),
            sample_cols(OFF_Q), sample_cols(OFF_K), sample_cols(OFF_V), sample_cols(OFF_SBG),
            *page_specs, *page_specs,
            pl.BlockSpec((NH * s, page), lambda b, h, i, pt: (0, 0)),
            pl.BlockSpec((page, 2 * page), lambda b, h, i, pt: (0, 0)),
        ],
        out_specs=[
            pl.BlockSpec((qb, LANE), lambda b, h, i, pt: (b * nq + i, h)),
            pl.BlockSpec((ns, SBW), lambda b, h, i, pt: (0, 0)),
        ],
        scratch_shapes=[pltpu.VMEM((2 * qb, 2 * LANE), BF16), pltpu.VMEM((2 * qb, LANE), F32),
                        pltpu.VMEM((2 * qb, LANE), F32),
                        pltpu.VMEM((2, group, NH * s, SBW), BF16), pltpu.VMEM((2, group, NH * s, SBW), F32),
                        pltpu.VMEM((2, group, NH * s, page), F32)],
    )
    return pl.pallas_call(
        functools.partial(_attn_kernel, nsub=nsub, n_pages=n_pages, group=group, units=units, n_units=n_units, s=s),
        grid_spec=grid_spec,
        out_shape=[jax.ShapeDtypeStruct((n, SBW), BF16), jax.ShapeDtypeStruct((ns, SBW), F32)],
        compiler_params=pltpu.CompilerParams(dimension_semantics=("arbitrary", "arbitrary", "arbitrary"),
                                             vmem_limit_bytes=ATTN_VMEM_LIMIT),
        name="attn",
    )(step_pages, q, kvt, kvt, u, bias_pairs, uo, u_s, u_s, u_s, u_s,
      *([cache_k] * (units * group)), *([cache_v] * (units * group)), bias_rows, uo_s)


def _conv_kernel(cin_ref, cg_ref, st_ref, w_ref, b_ref, lng_ref, lnb_ref, pw_ref, o_ref, stout_ref,
                 full_ref, shift_ref, y_ref, *, tb, sub, last_t, state_off):
    t = pl.program_id(1)

    @pl.when(t == 0)
    def _():
        full_ref[0:CPAD, :] = st_ref[...]

    cin = cin_ref[...]
    full_ref[CPAD:CPAD + tb, :] = cin[:, :CCH] * jax.nn.sigmoid(cin[:, CCH:])
    nshift = shift_ref.shape[1]
    for s in range(1, 8):
        shift_ref[s - 1] = full_ref[s:s + nshift, :]
    lead = CPAD - (CW - 1)
    for r0 in range(0, tb, sub):
        acc = jnp.zeros((sub, CCH), F32) + b_ref[...]
        for j in range(CW):
            a, s = divmod(lead + j, 8)
            lo = r0 + 8 * a
            rows = full_ref[lo:lo + sub, :] if s == 0 else shift_ref[s - 1, lo:lo + sub, :]
            acc = acc + jnp.concatenate([w_ref[8 * j:8 * j + 8, :]] * (sub // 8), axis=0) * rows
        y_ref[r0:r0 + sub, :] = acc
    c = y_ref[...]
    mu = jnp.mean(c, axis=-1, keepdims=True)
    d = c - mu
    var = jnp.mean(d * d, axis=-1, keepdims=True)
    c = d * lax.rsqrt(var + EPS) * lng_ref[...] + lnb_ref[...]
    c = _dot(_silu(c).astype(BF16), pw_ref[...])
    o_ref[...] = (c * _silu(cg_ref[...])).astype(o_ref.dtype)

    @pl.when(t == last_t)
    def _():
        stout_ref[...] = full_ref[state_off:state_off + CPAD, :]

    full_ref[0:CPAD, :] = full_ref[tb:tb + CPAD, :]


def _conv(u, ubase, state_pad, dw_w, dw_b, ln_g, ln_b, pw2_bf, layer, nseq, tpad, t_real, tb, out_dtype):
    nt = tpad // tb
    last_t = (t_real - 1) // tb
    state_off = t_real - last_t * tb
    sub = min(tb, 32)
    kern = functools.partial(_conv_kernel, tb=tb, sub=sub, last_t=last_t, state_off=state_off)
    vec = pl.BlockSpec((None, 1, CCH), lambda b, t: (layer, 0, 0))
    return pl.pallas_call(
        kern,
        grid=(nseq, nt),
        in_specs=[
            pl.BlockSpec((tb, 2 * CCH), lambda b, t: (b * nt + t, (OFF_CIN - ubase) // (2 * CCH))),
            pl.BlockSpec((tb, CCH), lambda b, t: (b * nt + t, (OFF_CG - ubase) // CCH)),
            pl.BlockSpec((None, CPAD, CCH), lambda b, t: (b, 0, 0)),
            pl.BlockSpec((None, 8 * CW, CCH), lambda b, t: (layer, 0, 0)),
            vec, vec, vec,
            pl.BlockSpec((None, CCH, CCH), lambda b, t: (layer, 0, 0)),
        ],
        out_specs=[
            pl.BlockSpec((tb, CCH), lambda b, t: (b * nt + t, 0)),
            pl.BlockSpec((None, CPAD, CCH), lambda b, t: (b, 0, 0)),
        ],
        out_shape=[
            jax.ShapeDtypeStruct((nseq * tpad, CCH), out_dtype),
            jax.ShapeDtypeStruct((nseq, CPAD, CCH), F32),
        ],
        scratch_shapes=[pltpu.VMEM((CPAD + tb, CCH), F32), pltpu.VMEM((7, tb + CPAD - 8, CCH), F32),
                        pltpu.VMEM((tb, CCH), F32)],
        compiler_params=_params(("arbitrary", "arbitrary")),
        name="conv",
    )(u, u, state_pad, dw_w, dw_b, ln_g, ln_b, pw2_bf)


def _gla_kernel(gq_ref, gk_ref, gv_ref, gg_ref, glr_ref, s0_ref, w2h_ref, w2l_ref, gkb_ref, ng_ref,
                ts_ref, gmat_ref, o_ref, sout_ref, s_ref, *, rows, t_real, n_steps):
    c = pl.program_id(1)
    nseq = gq_ref.shape[0]
    nch = max(rows // BLK, 1)
    crow = min(rows, BLK)
    items = [(g, ch) for g in range(nseq) for ch in range(nch)]
    idx = range(len(items))

    @pl.when(c == 0)
    def _():
        s_ref[...] = s0_ref[...]

    def chunk(ref, k):
        g, ch = items[k]
        x = ref[g, ch * BLK:ch * BLK + crow, :]
        if crow == BLK:
            return x
        return jnp.concatenate([x, jnp.zeros((BLK - crow, x.shape[1]), x.dtype)], axis=0)

    def rows_of(x, k):
        return x[k * BLK:(k + 1) * BLK]

    ridx = lax.broadcasted_iota(jnp.int32, (BLK, GKW), 0)
    valid_ch = [ridx < jnp.minimum(crow, t_real - c * rows - ch * BLK) for ch in range(nch)]
    valid = [valid_ch[ch] for _, ch in items]
    gq = [chunk(gq_ref, k) * (GDK ** -0.5) for k in idx]
    gk = [jnp.where(valid[k], chunk(gk_ref, k), 0.0) for k in idx]
    gv_bf = [chunk(gv_ref, k).astype(BF16) for k in idx]
    glr_h, glr_l = _split2(jnp.concatenate([chunk(glr_ref, k) for k in idx], axis=0))
    x = _dot(glr_h, w2h_ref[...]) + _dot(glr_l, w2h_ref[...]) + _dot(glr_h, w2l_ref[...]) + gkb_ref[...]
    lg_all = jnp.where(jnp.concatenate(valid, axis=0), _log_sigmoid(x) * (1.0 / GATE_NORM), 0.0)
    lg = [rows_of(lg_all, k) for k in idx]
    ts = ts_ref[...]
    lg_h, lg_m, lg_l = _split3(jnp.concatenate(lg, axis=1))
    br = _dot(ts, lg_h) + _dot(ts, lg_m) + _dot(ts, lg_l)
    b = [br[:BLK, k * GKW:(k + 1) * GKW] for k in idx]
    r = [br[BLK:, k * GKW:(k + 1) * GKW] for k in idx]
    qd = [gq[k] * jnp.exp(b[k] - r[k]) for k in idx]
    nst = GH * GSUB
    qmask = (lax.broadcasted_iota(jnp.int32, (nst, GKW), 1) // GDK) == (lax.broadcasted_iota(jnp.int32, (nst, GKW), 0) // GSUB)
    vmask = (lax.broadcasted_iota(jnp.int32, (nst, GVW), 1) // GDV) == (lax.broadcasted_iota(jnp.int32, (nst, GVW), 0) // GSUB)
    trow = lax.rem(lax.broadcasted_iota(jnp.int32, (nst, BLK), 0), GSUB)
    scol = lax.broadcasted_iota(jnp.int32, (nst, BLK), 1)
    nsc = BLK // GSUB
    att = [[] for _ in idx]
    for i in range(nsc):
        lo = i * GSUB
        kd = [(gk[k] * jnp.exp(r[k][lo:lo + 1, :] - b[k])).astype(BF16) for k in idx]
        qst = [jnp.where(qmask, jnp.concatenate([qd[k][lo:lo + GSUB]] * GH, axis=0), 0.0).astype(BF16) for k in idx]
        for k in idx:
            att[k].append(jnp.where(scol <= trow + lo, _dot_nt(qst[k], kd[k]), 0.0).astype(BF16))
    vmask_all = jnp.concatenate([vmask] * nsc, axis=0)
    ov = [jnp.where(vmask_all, _dot(jnp.concatenate(att[k], axis=0), gv_bf[k]), 0.0) for k in idx]
    kdec_t = [(gk[k] * jnp.exp(b[k][BLK - 1:BLK, :] - b[k])).T.astype(BF16) for k in idx]
    kv = [_dot(kdec_t[k], gv_bf[k]) for k in idx]
    bdmask = (lax.broadcasted_iota(jnp.int32, (GKW, GVW), 0) // GDK) == (lax.broadcasted_iota(jnp.int32, (GKW, GVW), 1) // GDV)
    s_cur = [s_ref[g] for g in range(nseq)]
    o_inter = [None] * len(items)
    for ch in range(nch):
        for g in range(nseq):
            k = g * nch + ch
            o_inter[k] = _dot((gq[k] * jnp.exp(b[k])).astype(BF16), s_cur[g].astype(BF16))
        for g in range(nseq):
            k = g * nch + ch
            bl_col = jnp.sum(lg[k].T, axis=1, keepdims=True)
            s_cur[g] = jnp.exp(bl_col) * s_cur[g] + jnp.where(bdmask, kv[k], 0.0)
    for g in range(nseq):
        s_ref[g] = s_cur[g]

    def heads_summed(x, i):
        parts = [x[(i * GH + h) * GSUB:(i * GH + h + 1) * GSUB] for h in range(GH)]
        return functools.reduce(lambda acc, part: acc + part, parts)

    o = jnp.concatenate([jnp.concatenate([heads_summed(ov[k], i) for i in range(nsc)], axis=0) + o_inter[k]
                         for k in idx], axis=0)
    o2_h, o2_l = _split2(o * o)
    ms = _dot(o2_h, gmat_ref[...]) + _dot(o2_l, gmat_ref[...])
    gate = jnp.concatenate([chunk(gg_ref, k) for k in idx], axis=0)
    res = o * lax.rsqrt(ms + EPS) * ng_ref[...] * _silu(gate)
    for k, (g, ch) in enumerate(items):
        o_ref[g, ch * BLK:ch * BLK + crow, :] = rows_of(res, k)[:crow].astype(o_ref.dtype)

    @pl.when(c == n_steps - 1)
    def _():
        sout_ref[...] = s_ref[...]


def _gla(u, ubase, s0_bd, w2h, w2l, gkb, ng_tiled, ts, gmat, layer, nseq, tpad, t_real, rows, group, out_dtype):
    nc = tpad // rows
    u3 = u.reshape(nseq, tpad, u.shape[1])
    kern = functools.partial(_gla_kernel, rows=rows, t_real=t_real, n_steps=nc)

    def cols(off, width):
        return pl.BlockSpec((group, rows, width), lambda b, c: (b, c, (off - ubase) // width))

    state_spec = pl.BlockSpec((group, GKW, GVW), lambda b, c: (b, 0, 0))
    o, s_out = pl.pallas_call(
        kern,
        grid=(nseq // group, nc),
        in_specs=[
            cols(OFF_GQ, GKW), cols(OFF_GK, GKW), cols(OFF_GV, GVW), cols(OFF_GG, GVW), cols(OFF_GLR, LANE),
            state_spec,
            pl.BlockSpec((None, LANE, GKW), lambda b, c: (layer, 0, 0)),
            pl.BlockSpec((None, LANE, GKW), lambda b, c: (layer, 0, 0)),
            pl.BlockSpec((None, 1, GKW), lambda b, c: (layer, 0, 0)),
            pl.BlockSpec((None, 1, GVW), lambda b, c: (layer, 0, 0)),
            pl.BlockSpec((2 * BLK, BLK), lambda b, c: (0, 0)),
            pl.BlockSpec((GVW, GVW), lambda b, c: (0, 0)),
        ],
        out_specs=[
            pl.BlockSpec((group, rows, GVW), lambda b, c: (b, c, 0)),
            state_spec,
        ],
        out_shape=[
            jax.ShapeDtypeStruct((nseq, tpad, GVW), out_dtype),
            jax.ShapeDtypeStruct((nseq, GKW, GVW), F32),
        ],
        scratch_shapes=[pltpu.VMEM((group, GKW, GVW), F32)],
        compiler_params=_params(("arbitrary", "arbitrary")),
        name="gla",
    )(u3, u3, u3, u3, u3, s0_bd, w2h, w2l, gkb, ng_tiled, ts, gmat)
    return o.reshape(nseq * tpad, GVW), s_out


def _state_to_blockdiag(s):
    n = s.shape[0]
    out = jnp.zeros((n, GH, GDK, GH, GDV), F32)
    for h in range(GH):
        out = out.at[:, h, :, h, :].set(s[:, h].astype(F32))
    return out.reshape(n, GKW, GVW)


def _blockdiag_to_state(s_bd):
    n = s_bd.shape[0]
    s5 = s_bd.reshape(n, GH, GDK, GH, GDV)
    return jnp.stack([s5[:, h, :, h, :] for h in range(GH)], axis=1)


def _row_block(n):
    for bm in (512, 256, 128, 64, 32, 16, 8):
        if n % bm == 0:
            return bm
    raise ValueError(f"row count {n} is not a multiple of 8")


def kernel(x_prompt, x_sample, cache_k, cache_v, state_conv, state_gla, page_table, meta_tokens, norm_g, w_in,
           sb_bias, conv_dw_w, conv_dw_b, conv_ln_g, conv_ln_b, conv_pw2, gla_gk_w2, gla_gk_b, gla_norm_g, w_out,
           final_norm_g):
    bp, seq, _ = x_prompt.shape
    bd, dseq, _ = x_sample.shape
    depth = w_in.shape[0]
    n_pool, page = cache_k.shape[1], cache_k.shape[2]
    assert page == BLK and dseq % 8 == 0 and BLK % dseq == 0
    t_real = N_META + seq
    tpad = -(-t_real // KBLK) * KBLK

    w_in_bf = jnp.pad(w_in, ((0, 0), (0, 0), (0, IN_WP - IN_W))).astype(BF16)
    wkv_t = jnp.transpose(w_in[:, :, OFF_K:OFF_SBG], (0, 2, 1)).astype(BF16)
    w_out_bf = w_out.astype(BF16)
    norm_g3 = norm_g.reshape(depth, 1, D_MODEL)
    dw_w = jnp.repeat(conv_dw_w, 8, axis=1)
    dw_b = conv_dw_b.reshape(depth, 1, CCH)
    ln_g = conv_ln_g.reshape(depth, 1, CCH)
    ln_b = conv_ln_b.reshape(depth, 1, CCH)
    pw2_bf = conv_pw2.astype(BF16)
    w2 = jnp.pad(gla_gk_w2, ((0, 0), (0, LANE - GRANK), (0, 0)))
    w2h = w2.astype(BF16)
    w2l = (w2 - w2h.astype(F32)).astype(BF16)
    gkb = gla_gk_b.reshape(depth, 1, GKW)
    ng_tiled = jnp.tile(gla_norm_g, (1, GH)).reshape(depth, 1, GVW)
    ii = jnp.arange(BLK)
    uo = jnp.concatenate([(ii[:, None] > ii[None, :]), jnp.ones((BLK, LANE), bool)], axis=1).astype(BF16)
    ik = jnp.arange(KBLK)
    uo_p = (ik[:, None] > ik[None, :]).astype(BF16)
    tri = ii[None, :] <= ii[:, None]
    sel = ii[None, :] < (ii[:, None] // GSUB) * GSUB
    ts = jnp.concatenate([tri, sel], axis=0).astype(BF16)
    gi = jnp.arange(GVW) // GDV
    gmat = ((gi[:, None] == gi[None, :]).astype(F32) / GDV).astype(BF16)
    final_g = final_norm_g.reshape(1, D_MODEL)

    meta = jnp.broadcast_to(meta_tokens[None].astype(F32), (bp, N_META, D_MODEL))
    xp = jnp.concatenate([meta, x_prompt, jnp.zeros((bp, tpad - t_real, D_MODEL), F32)], axis=1)
    xp = xp.reshape(bp * tpad, D_MODEL)
    xs = x_sample.reshape(bd * dseq, D_MODEL)
    bm_p = _row_block(bp * tpad)
    bm_s = _row_block(bd * dseq)
    conv_tb = KBLK if tpad % KBLK == 0 else BLK
    nsub = max(d for d in (2, 4, 6) if (tpad // BLK) % d == 0)
    group = max(d for d in range(1, 9) if bd % d == 0)
    gla_group_p = max(d for d in range(1, 5) if bp % d == 0)
    gla_group_s = max(d for d in range(1, 5) if bd % d == 0)
    cache_k4 = jnp.transpose(cache_k, (0, 1, 3, 4, 2)).reshape(depth, n_pool, SBW, page)
    cache_v4 = jnp.transpose(cache_v, (0, 1, 3, 4, 2)).reshape(depth, n_pool, SBW, page)
    zero_conv = jnp.zeros((bp, CPAD, CCH), F32)
    zero_gla = jnp.zeros((bp, GKW, GVW), F32)

    cp_l, sp_l, ks_l, vs_l, cs_l, ss_l = ([] for _ in range(6))
    kt_all = vt_all = None
    for l in range(depth):
        bias2 = sb_bias[l] * LOG2E
        bias_rows_p = jnp.tile(jnp.repeat(bias2.reshape(NH // 2, 2), BLK, axis=1), (1, nsub))
        bias_pairs = jnp.pad(jnp.stack(_split3(bias_rows_p), axis=-1), ((0, 0), (0, 0), (0, LANE - N_BIAS_PIECES)))
        bias_rows = jnp.broadcast_to(jnp.repeat(bias2, dseq)[:, None], (NH * dseq, page))
        last = l == depth - 1
        u_p, q_p, kt_all, vt_all, kvt_p = _inproj_prompt(xp, norm_g3, w_in_bf, wkv_t, kt_all, vt_all, l, bp, tpad,
                                                         t_real, depth, KBLK)
        u_s = _inproj(xs, norm_g3, w_in_bf, l, bm_s)
        mix_sb, smix_sb = _attn(q_p, kvt_p, u_p, OFF_SBG, bias_pairs, uo_p, bp, tpad, nsub, u_s, cache_k4, cache_v4,
                                page_table, bias_rows, uo, l, bd, dseq, group)
        mix_cv, cst_p = _conv(u_p, OFF_SBG, zero_conv, dw_w, dw_b, ln_g, ln_b, pw2_bf, l, bp, tpad, t_real, conv_tb,
                              BF16)
        mix_gla, gst_p = _gla(u_p, OFF_SBG, zero_gla, w2h, w2l, gkb, ng_tiled, ts, gmat, l, bp, tpad, t_real, KBLK,
                              gla_group_p, BF16)
        xp = _outproj(xp, mix_sb, mix_cv, mix_gla, w_out_bf, l, bm_p, final_g if last else None)
        cp_l.append(cst_p[:, CPAD - (CW - 1):])
        sp_l.append(_blockdiag_to_state(gst_p))
        st_s = jnp.pad(state_conv[l], ((0, 0), (CPAD - (CW - 1), 0), (0, 0)))
        smix_cv, cst_s = _conv(u_s, 0, st_s, dw_w, dw_b, ln_g, ln_b, pw2_bf, l, bd, dseq, dseq, dseq, F32)
        smix_gla, gst_s = _gla(u_s, 0, _state_to_blockdiag(state_gla[l]), w2h, w2l, gkb, ng_tiled, ts, gmat, l, bd,
                               dseq, dseq, dseq, gla_group_s, F32)
        xs = _outproj(xs, smix_sb, smix_cv, smix_gla, w_out_bf, l, bm_s, final_g if last else None)
        ks_l.append(u_s[:, OFF_K:OFF_V].reshape(bd, dseq, NH, DH))
        vs_l.append(u_s[:, OFF_V:OFF_SBG].reshape(bd, dseq, NH, DH))
        cs_l.append(cst_s[:, CPAD - (CW - 1):])
        ss_l.append(_blockdiag_to_state(gst_s))

    y_prompt = xp.reshape(bp, tpad, D_MODEL)[:, N_META:t_real]
    y_sample = xs.reshape(bd, dseq, D_MODEL)
    new_k_prompt = jnp.transpose(kt_all[..., :t_real].reshape(depth, bp, NH, DH, t_real), (0, 1, 4, 2, 3))
    new_v_prompt = jnp.transpose(vt_all[..., :t_real].reshape(depth, bp, NH, DH, t_real), (0, 1, 4, 2, 3))
    return (y_prompt, y_sample, new_k_prompt, new_v_prompt, jnp.stack(cp_l), jnp.stack(sp_l),
            jnp.stack(ks_l), jnp.stack(vs_l), jnp.stack(cs_l), jnp.stack(ss_l))
```

```python
import functools

import jax
import jax.numpy as jnp
from jax import lax
from jax.experimental import pallas as pl
from jax.experimental.pallas import tpu as pltpu

F32 = jnp.float32
BF16 = jnp.bfloat16

D_MODEL = 1024
N_META = 16
NH = 8
DH = 64
SBW = NH * DH
CCH = 256
CW = 31
GH = 4
GDK = 32
GDV = 64
GKW = GH * GDK
GVW = GH * GDV
GRANK = 16
GATE_NORM = 16.0
EPS = 1e-6

LANE = 128
MXU_DIM = 256
BLK = 128
KBLK = MXU_DIM
SPANS_PER_TRIP = 3
LOG2E = 1.4426950408889634
Q_SCALE = DH ** -0.5 * LOG2E
N_BIAS_PIECES = 3
GSUB = 32
CPAD = 32
IN_W = 3600
IN_WP = 3712
OFF_Q, OFF_K, OFF_V, OFF_SBG = 0, 512, 1024, 1536
OFF_CIN, OFF_CG = 2048, 2560
OFF_GQ, OFF_GK, OFF_GV, OFF_GG, OFF_GLR = 2816, 2944, 3072, 3328, 3584
VMEM_LIMIT = 48 * 1024 * 1024
ATTN_VMEM_LIMIT = 56 * 1024 * 1024


def _dot(a, b):
    return jnp.dot(a, b, preferred_element_type=F32)


def _dot_nt(a, b):
    return lax.dot_general(a, b, (((1,), (1,)), ((), ())), preferred_element_type=F32)


def _split2(x):
    hi = x.astype(BF16)
    lo = (x - hi.astype(F32)).astype(BF16)
    return hi, lo


def _split3(x):
    hi = x.astype(BF16)
    r = x - hi.astype(F32)
    mid = r.astype(BF16)
    lo = (r - mid.astype(F32)).astype(BF16)
    return hi, mid, lo


def _log_sigmoid(x):
    return jnp.minimum(x, 0.0) - jnp.log(1.0 + jnp.exp(-jnp.abs(x)))


def _silu(x):
    return x * jax.nn.sigmoid(x)


def _params(sem):
    return pltpu.CompilerParams(dimension_semantics=sem, vmem_limit_bytes=VMEM_LIMIT)


def _inproj_kernel(x_ref, g_ref, w_ref, u_ref):
    x = x_ref[...]
    ms = jnp.mean(x * x, axis=-1, keepdims=True)
    h = (x * lax.rsqrt(ms + EPS) * g_ref[...]).astype(BF16)
    for c0 in range(0, IN_WP, SBW):
        wd = min(SBW, IN_WP - c0)
        u_ref[:, c0:c0 + wd] = _dot(h, w_ref[:, c0:c0 + wd])


def _inproj_prompt_kernel(x_ref, g_ref, w_ref, wkv_ref, *rest):
    u_ref, q_ref, kt_ref, vt_ref, kvbf_ref = rest[-5:]
    x = x_ref[...]
    ms = jnp.mean(x * x, axis=-1, keepdims=True)
    h = (x * lax.rsqrt(ms + EPS) * g_ref[...]).astype(BF16)
    q_ref[...] = (_dot(h, w_ref[:, OFF_Q:OFF_Q + SBW]) * Q_SCALE).astype(BF16)
    for c0 in range(OFF_SBG, IN_WP, SBW):
        wd = min(SBW, IN_WP - c0)
        u_ref[:, c0 - OFF_SBG:c0 - OFF_SBG + wd] = _dot(h, w_ref[:, c0:c0 + wd])
    kvt = _dot_nt(wkv_ref[...], h)
    kt_ref[...] = kvt[:SBW]
    vt_ref[...] = kvt[SBW:]
    kvbf_ref[...] = kvt.astype(BF16)


def _inproj_prompt(x, norm_g, w_in_bf, wkv_t, kt_prev, vt_prev, layer, nseq, tpad, t_real, depth, bm):
    n = x.shape[0]
    nt = tpad // bm
    in_specs = [
        pl.BlockSpec((bm, D_MODEL), lambda b, t: (b * nt + t, 0)),
        pl.BlockSpec((None, 1, D_MODEL), lambda b, t: (layer, 0, 0)),
        pl.BlockSpec((None, D_MODEL, IN_WP), lambda b, t: (layer, 0, 0)),
        pl.BlockSpec((None, 2 * SBW, D_MODEL), lambda b, t: (layer, 0, 0)),
    ]
    args = [x, norm_g, w_in_bf, wkv_t]
    aliases = {}
    if kt_prev is not None:
        in_specs += [pl.BlockSpec(memory_space=pl.ANY), pl.BlockSpec(memory_space=pl.ANY)]
        args += [kt_prev, vt_prev]
        aliases = {4: 2, 5: 3}
    cache = jax.ShapeDtypeStruct((depth, nseq, SBW, tpad), F32)
    cache_spec = pl.BlockSpec((None, None, SBW, bm), lambda b, t: (layer, b, 0, t))
    return pl.pallas_call(
        _inproj_prompt_kernel,
        grid=(nseq, nt),
        in_specs=in_specs,
        out_specs=[
            pl.BlockSpec((bm, IN_WP - OFF_SBG), lambda b, t: (b * nt + t, 0)),
            pl.BlockSpec((bm, SBW), lambda b, t: (b * nt + t, 0)),
            cache_spec, cache_spec,
            pl.BlockSpec((None, 2 * SBW, bm), lambda b, t: (b, 0, t)),
        ],
        out_shape=[
            jax.ShapeDtypeStruct((n, IN_WP - OFF_SBG), F32),
            jax.ShapeDtypeStruct((n, SBW), BF16),
            cache, cache,
            jax.ShapeDtypeStruct((nseq, 2 * SBW, tpad), BF16),
        ],
        input_output_aliases=aliases,
        compiler_params=_params(("parallel", "parallel")),
        name="inproj_prompt",
    )(*args)


def _inproj(x, norm_g, w_in_bf, layer, bm):
    n = x.shape[0]
    return pl.pallas_call(
        _inproj_kernel,
        grid=(n // bm,),
        in_specs=[
            pl.BlockSpec((bm, D_MODEL), lambda i: (i, 0)),
            pl.BlockSpec((None, 1, D_MODEL), lambda i: (layer, 0, 0)),
            pl.BlockSpec((None, D_MODEL, IN_WP), lambda i: (layer, 0, 0)),
        ],
        out_specs=pl.BlockSpec((bm, IN_WP), lambda i: (i, 0)),
        out_shape=jax.ShapeDtypeStruct((n, IN_WP), F32),
        compiler_params=_params(("parallel",)),
        name="inproj",
    )(x, norm_g, w_in_bf)


def _outproj_kernel(x_ref, a_ref, b_ref, c_ref, w_ref, *rest, final):
    y = x_ref[...]
    y = y + _dot(a_ref[...].astype(BF16), w_ref[0:SBW, :])
    y = y + _dot(b_ref[...].astype(BF16), w_ref[SBW:SBW + CCH, :])
    y = y + _dot(c_ref[...].astype(BF16), w_ref[SBW + CCH:, :])
    if final:
        g_ref, y_ref = rest
        ms = jnp.mean(y * y, axis=-1, keepdims=True)
        y_ref[...] = y * lax.rsqrt(ms + EPS) * g_ref[...]
    else:
        (y_ref,) = rest
        y_ref[...] = y


def _outproj(x, mix_sb, mix_cv, mix_gla, w_out_bf, layer, bm, final_g=None):
    n = x.shape[0]
    final = final_g is not None
    in_specs = [
        pl.BlockSpec((bm, D_MODEL), lambda i: (i, 0)),
        pl.BlockSpec((bm, SBW), lambda i: (i, 0)),
        pl.BlockSpec((bm, CCH), lambda i: (i, 0)),
        pl.BlockSpec((bm, GVW), lambda i: (i, 0)),
        pl.BlockSpec((None, D_MODEL, D_MODEL), lambda i: (layer, 0, 0)),
    ]
    args = [x, mix_sb, mix_cv, mix_gla, w_out_bf]
    if final:
        in_specs.append(pl.BlockSpec((1, D_MODEL), lambda i: (0, 0)))
        args.append(final_g)
    return pl.pallas_call(
        functools.partial(_outproj_kernel, final=final),
        grid=(n // bm,),
        in_specs=in_specs,
        out_specs=pl.BlockSpec((bm, D_MODEL), lambda i: (i, 0)),
        out_shape=jax.ShapeDtypeStruct((n, D_MODEL), F32),
        compiler_params=_params(("parallel",)),
        name="outproj",
    )(*args)


def _neg_abs(x):
    bits = lax.bitcast_convert_type(x, jnp.uint32) | jnp.uint32(0x80000000)
    return lax.bitcast_convert_type(bits, F32)


def _sb_logs(z):
    l = jnp.log(1.0 + jnp.exp2(_neg_abs(z))) * LOG2E
    ls = jnp.minimum(z, 0.0) - l
    return ls, ls - z


def _lane_tile(x, n):
    reps = n // x.shape[1]
    return x if reps == 1 else jnp.concatenate([x] * reps, axis=1)


def _sb_weights(ls, cr, carry, mask):
    n = ls.shape[1]
    w = jnp.exp2(ls + cr[:, :n] + _lane_tile(carry, n))
    if mask is not None:
        w = jnp.where(mask, w, 0.0)
    return w.astype(BF16)


def _sb_block(qs, kb, vb, uo, carry_ref, acc_ref, mask, r0=0):
    ls, lf = _sb_logs(_dot(qs, kb))
    if mask is not None:
        lf = jnp.where(mask, lf, 0.0)
    lfb = lf.astype(BF16)
    cr = _dot(lfb, uo)
    acc_ref[r0:, :] += _dot_nt(_sb_weights(ls, cr, carry_ref[r0:, :], mask), vb)
    tot = cr[:, 0:1] + lfb[:, 0:1].astype(F32)
    carry_ref[r0:, :] += jnp.broadcast_to(tot, (tot.shape[0], LANE))


def _sb_blocks_staged(qs, kb, vb, bias, uo, carry_ref, acc_ref, mask, transposed):
    m = qs[0].shape[0]
    n = uo.shape[0]
    logs = [_sb_logs((_dot(q, k) if transposed else _dot_nt(q, k)) + bias) for q, k in zip(qs, kb)]
    lfs = [(lf if mask is None else jnp.where(mask, lf, 0.0)).astype(BF16) for _, lf in logs]
    cr_all = _dot(jnp.concatenate(lfs, axis=0), uo)
    for g in range(len(qs)):
        cr = cr_all[g * m:(g + 1) * m]
        w = _sb_weights(logs[g][0], cr, carry_ref[g], mask)
        acc_ref[g] += _dot_nt(w, vb[g]) if transposed else _dot(w, vb[g])
        carry_ref[g] += cr[:, n:]


class _SampleUnit:
    def __init__(self, j, n_units, n_pages, s, q_ref, kn_ref, vn_ref, gate_ref, kp_refs, vp_refs, bias_ref, uo_ref,
                 o_ref, qs_ref, acc_ref, carry_ref):
        self.group = len(kp_refs)
        self.s, self.n_pages = s, n_pages
        self.m = NH * s
        self.page = uo_ref.shape[0]
        self.gi = j // (n_pages + 1)
        self.ps = j - self.gi * (n_pages + 1)
        self.live = j < n_units
        slot = self.gi & 1
        self.q_ref, self.kn_ref, self.vn_ref, self.gate_ref = q_ref, kn_ref, vn_ref, gate_ref
        self.kp_refs, self.vp_refs, self.bias_ref, self.uo_ref, self.o_ref = kp_refs, vp_refs, bias_ref, uo_ref, o_ref
        self.qs_ref, self.acc_ref, self.carry_ref = qs_ref.at[slot], acc_ref.at[slot], carry_ref.at[slot]

    def seq_rows(self, g):
        return pl.ds(pl.multiple_of((self.gi * self.group + g) * self.s, self.s), self.s)

    def start(self):
        s, m, page, group = self.s, self.m, self.page, self.group

        @pl.when(jnp.logical_and(self.live, self.ps == 0))
        def _():
            head = lax.broadcasted_iota(jnp.int32, (s, SBW), 1) // DH
            row = lax.rem(lax.broadcasted_iota(jnp.int32, (m, page), 0), s)
            col = lax.broadcasted_iota(jnp.int32, (m, page), 1)
            pad = jnp.zeros((page - s, SBW), F32)
            self.acc_ref[...] = jnp.zeros_like(self.acc_ref)
            self.carry_ref[...] = jnp.zeros_like(self.carry_ref)
            qs, kb, vb = [], [], []
            for g in range(group):
                q = self.q_ref[self.seq_rows(g), :] * Q_SCALE
                qs.append(jnp.concatenate([jnp.where(head == h, q, 0.0) for h in range(NH)], axis=0).astype(BF16))
                self.qs_ref[g] = qs[g]
                kb.append(jnp.concatenate([self.kn_ref[self.seq_rows(g), :], pad], axis=0).astype(BF16))
                vb.append(jnp.concatenate([self.vn_ref[self.seq_rows(g), :], pad], axis=0).astype(BF16))
            _sb_blocks_staged(qs, kb, vb, self.bias_ref[...], self.uo_ref[...], self.carry_ref, self.acc_ref,
                              col < row, False)

    def page_step(self):
        group = self.group
        on = jnp.where(jnp.logical_and(self.live, self.ps > 0), 1, 0)
        mask = jnp.broadcast_to(on, (self.m, self.page)) > 0
        _sb_blocks_staged([self.qs_ref[g] for g in range(group)],
                          [self.kp_refs[g][...].astype(BF16) for g in range(group)],
                          [self.vp_refs[g][...].astype(BF16) for g in range(group)],
                          self.bias_ref[...], self.uo_ref[...], self.carry_ref, self.acc_ref, mask, True)

    def finish(self):
        s = self.s

        @pl.when(jnp.logical_and(self.live, self.ps == self.n_pages))
        def _():
            head = lax.broadcasted_iota(jnp.int32, (s, SBW), 1) // DH
            for g in range(self.group):
                o = jnp.zeros((s, SBW), F32)
                for h in range(NH):
                    o = jnp.where(head == h, self.acc_ref[g, h * s:(h + 1) * s, :], o)
                self.o_ref[self.seq_rows(g), :] = o * _silu(self.gate_ref[self.seq_rows(g), :])


def _attn_kernel(pt_ref, q_ref, k_ref, v_ref, gate_ref, bias_ref, uo_ref, sq_ref, skn_ref, svn_ref, sgate_ref, *rest,
                 nsub, n_pages, group, units, n_units, s):
    del pt_ref
    npg = units * group
    kp_refs, vp_refs = rest[:npg], rest[npg:2 * npg]
    sbias_ref, suo_ref, o_ref, so_ref, qs_ref, acc_ref, carry_ref, sqs_ref, sacc_ref, scarry_ref = rest[2 * npg:]
    step = (pl.program_id(0) * pl.num_programs(1) + pl.program_id(1)) * pl.num_programs(2) + pl.program_id(2)
    sample_units = [
        _SampleUnit(step * units + un, n_units, n_pages, s, sq_ref, skn_ref, svn_ref, sgate_ref,
                    kp_refs[un * group:(un + 1) * group], vp_refs[un * group:(un + 1) * group], sbias_ref, suo_ref,
                    so_ref, sqs_ref, sacc_ref, scarry_ref)
        for un in range(units)]
    for unit in sample_units:
        unit.start()
    for unit in sample_units:
        unit.page_step()

    qi = pl.program_id(2)
    qb = nsub * BLK
    lane = lax.broadcasted_iota(jnp.int32, (BLK, LANE), 1)
    for g in range(nsub):
        q = q_ref[g * BLK:(g + 1) * BLK, :]
        zero = jnp.zeros_like(q)
        qs_ref[2 * g * BLK:(2 * g + 1) * BLK, :LANE] = jnp.where(lane < DH, q, zero)
        qs_ref[(2 * g + 1) * BLK:(2 * g + 2) * BLK, :LANE] = jnp.where(lane >= DH, q, zero)
    qs_ref[:, LANE:] = bias_ref[...]
    ones_rows = jnp.where(lax.broadcasted_iota(jnp.int32, (LANE, KBLK), 0) < N_BIAS_PIECES, 1.0, 0.0).astype(BF16)
    acc_ref[...] = jnp.zeros_like(acc_ref)
    carry_ref[...] = jnp.zeros_like(carry_ref)
    uo = uo_ref[...]
    for c in reversed(range(qb // KBLK)):
        r0 = 2 * c * KBLK
        m = 2 * qb - r0
        row = lax.broadcasted_iota(jnp.int32, (m, KBLK), 0)
        col = lax.broadcasted_iota(jnp.int32, (m, KBLK), 1)
        first_hidden = (row // (2 * BLK)) * BLK + (row & (BLK - 1))
        mask = col < jnp.where(row >= 2 * KBLK, KBLK, first_hidden)
        ks = pl.multiple_of(qi * qb + c * KBLK, KBLK)
        kb = jnp.concatenate([k_ref[:, pl.ds(ks, KBLK)], ones_rows], axis=0)
        _sb_block(qs_ref[r0:, :], kb, v_ref[:, pl.ds(ks, KBLK)], uo, carry_ref, acc_ref, mask, r0=r0)

    for unit in sample_units:
        unit.finish()

    def older_span(kbase):
        for c in reversed(range(qb // KBLK)):
            ks = pl.multiple_of(kbase + c * KBLK, KBLK)
            kb = jnp.concatenate([k_ref[:, pl.ds(ks, KBLK)], ones_rows], axis=0)
            _sb_block(qs_ref[...], kb, v_ref[:, pl.ds(ks, KBLK)], uo, carry_ref, acc_ref, None)

    trips = qi // SPANS_PER_TRIP

    def body(t, carry):
        kbase = (qi - 1 - SPANS_PER_TRIP * t) * qb
        for i in range(SPANS_PER_TRIP):
            older_span(kbase - i * qb)
        return carry

    lax.fori_loop(0, trips, body, 0)
    left = qi - SPANS_PER_TRIP * trips
    for r in range(1, SPANS_PER_TRIP):
        @pl.when(left == r)
        def _(r=r):
            for i in reversed(range(r)):
                older_span(i * qb)

    for g in range(nsub):
        o = jnp.where(lane < DH, acc_ref[2 * g * BLK:(2 * g + 1) * BLK, :],
                      acc_ref[(2 * g + 1) * BLK:(2 * g + 2) * BLK, :])
        o_ref[g * BLK:(g + 1) * BLK, :] = (o * _silu(gate_ref[g * BLK:(g + 1) * BLK, :])).astype(o_ref.dtype)


def _attn(q, kvt, u, ubase, bias_pairs, uo, nseq, tpad, nsub, u_s, cache_k, cache_v, page_table, bias_rows, uo_s,
          layer, nseq_s, s, group):
    n = nseq * tpad
    qb = nsub * BLK
    nq = tpad // qb
    hp = NH // 2
    gcol = (OFF_SBG - ubase) // LANE
    n_pages = page_table.shape[1]
    page = cache_k.shape[3]
    n_units = (nseq_s // group) * (n_pages + 1)
    n_steps = nseq * hp * nq
    units = -(-n_units // n_steps)
    ns = nseq_s * s

    j = jnp.minimum(jnp.arange(n_steps)[:, None, None] * units + jnp.arange(units)[None, :, None], n_units - 1)
    gi = j // (n_pages + 1)
    ps = j - gi * (n_pages + 1)
    step_pages = page_table[gi * group + jnp.arange(group)[None, None, :], n_pages - jnp.maximum(ps, 1)]
    step_pages = step_pages.reshape(n_steps, units * group)

    def page_map(un, g):
        return lambda b, h, i, pt: (layer, pt[(b * hp + h) * nq + i, un * group + g], 0, 0)

    page_specs = [pl.BlockSpec((None, None, SBW, page), page_map(un, g)) for un in range(units) for g in range(group)]

    def sample_cols(off):
        return pl.BlockSpec((ns, SBW), lambda b, h, i, pt: (0, off // SBW))

    grid_spec = pltpu.PrefetchScalarGridSpec(
        num_scalar_prefetch=1,
        grid=(nseq, hp, nq),
        in_specs=[
            pl.BlockSpec((qb, LANE), lambda b, h, i, pt: (b * nq + i, h)),
            pl.BlockSpec((None, LANE, tpad), lambda b, h, i, pt: (b, h, 0), pipeline_mode=pl.Buffered(1)),
            pl.BlockSpec((None, LANE, tpad), lambda b, h, i, pt: (b, hp + h, 0), pipeline_mode=pl.Buffered(1)),
            pl.BlockSpec((qb, LANE), lambda b, h, i, pt: (b * nq + i, gcol + h)),
            pl.BlockSpec((None, 2 * qb, LANE), lambda b, h, i, pt: (h, 0, 0)),
            pl.BlockSpec((KBLK, KBLK), lambda b, h, i, pt: (0, 0)),
            sample_cols(OFF_Q), sample_cols(OFF_K), sample_cols(OFF_V), sample_cols(OFF_SBG),
            *page_specs, *page_specs,
            pl.BlockSpec((NH * s, page), lambda b, h, i, pt: (0, 0)),
            pl.BlockSpec((page, 2 * page), lambda b, h, i, pt: (0, 0)),
        ],
        out_specs=[
            pl.BlockSpec((qb, LANE), lambda b, h, i, pt: (b * nq + i, h)),
            pl.BlockSpec((ns, SBW), lambda b, h, i, pt: (0, 0)),
        ],
        scratch_shapes=[pltpu.VMEM((2 * qb, 2 * LANE), BF16), pltpu.VMEM((2 * qb, LANE), F32),
                        pltpu.VMEM((2 * qb, LANE), F32),
                        pltpu.VMEM((2, group, NH * s, SBW), BF16), pltpu.VMEM((2, group, NH * s, SBW), F32),
                        pltpu.VMEM((2, group, NH * s, page), F32)],
    )
    return pl.pallas_call(
        functools.partial(_attn_kernel, nsub=nsub, n_pages=n_pages, group=group, units=units, n_units=n_units, s=s),
        grid_spec=grid_spec,
        out_shape=[jax.ShapeDtypeStruct((n, SBW), BF16), jax.ShapeDtypeStruct((ns, SBW), F32)],
        compiler_params=pltpu.CompilerParams(dimension_semantics=("arbitrary", "arbitrary", "arbitrary"),
                                             vmem_limit_bytes=ATTN_VMEM_LIMIT),
        name="attn",
    )(step_pages, q, kvt, kvt, u, bias_pairs, uo, u_s, u_s, u_s, u_s,
      *([cache_k] * (units * group)), *([cache_v] * (units * group)), bias_rows, uo_s)


def _conv_kernel(cin_ref, cg_ref, st_ref, w_ref, b_ref, lng_ref, lnb_ref, pw_ref, o_ref, stout_ref,
                 full_ref, shift_ref, y_ref, *, tb, sub, last_t, state_off):
    t = pl.program_id(1)

    @pl.when(t == 0)
    def _():
        full_ref[0:CPAD, :] = st_ref[...]

    cin = cin_ref[...]
    full_ref[CPAD:CPAD + tb, :] = cin[:, :CCH] * jax.nn.sigmoid(cin[:, CCH:])
    nshift = shift_ref.shape[1]
    for s in range(1, 8):
        shift_ref[s - 1] = full_ref[s:s + nshift, :]
    lead = CPAD - (CW - 1)
    for r0 in range(0, tb, sub):
        acc = jnp.zeros((sub, CCH), F32) + b_ref[...]
        for j in range(CW):
            a, s = divmod(lead + j, 8)
            lo = r0 + 8 * a
            rows = full_ref[lo:lo + sub, :] if s == 0 else shift_ref[s - 1, lo:lo + sub, :]
            acc = acc + jnp.concatenate([w_ref[8 * j:8 * j + 8, :]] * (sub // 8), axis=0) * rows
        y_ref[r0:r0 + sub, :] = acc
    c = y_ref[...]
    mu = jnp.mean(c, axis=-1, keepdims=True)
    d = c - mu
    var = jnp.mean(d * d, axis=-1, keepdims=True)
    c = d * lax.rsqrt(var + EPS) * lng_ref[...] + lnb_ref[...]
    c = _dot(_silu(c).astype(BF16), pw_ref[...])
    o_ref[...] = (c * _silu(cg_ref[...])).astype(o_ref.dtype)

    @pl.when(t == last_t)
    def _():
        stout_ref[...] = full_ref[state_off:state_off + CPAD, :]

    full_ref[0:CPAD, :] = full_ref[tb:tb + CPAD, :]


def _conv(u, ubase, state_pad, dw_w, dw_b, ln_g, ln_b, pw2_bf, layer, nseq, tpad, t_real, tb, out_dtype):
    nt = tpad // tb
    last_t = (t_real - 1) // tb
    state_off = t_real - last_t * tb
    sub = min(tb, 32)
    kern = functools.partial(_conv_kernel, tb=tb, sub=sub, last_t=last_t, state_off=state_off)
    vec = pl.BlockSpec((None, 1, CCH), lambda b, t: (layer, 0, 0))
    return pl.pallas_call(
        kern,
        grid=(nseq, nt),
        in_specs=[
            pl.BlockSpec((tb, 2 * CCH), lambda b, t: (b * nt + t, (OFF_CIN - ubase) // (2 * CCH))),
            pl.BlockSpec((tb, CCH), lambda b, t: (b * nt + t, (OFF_CG - ubase) // CCH)),
            pl.BlockSpec((None, CPAD, CCH), lambda b, t: (b, 0, 0)),
            pl.BlockSpec((None, 8 * CW, CCH), lambda b, t: (layer, 0, 0)),
            vec, vec, vec,
            pl.BlockSpec((None, CCH, CCH), lambda b, t: (layer, 0, 0)),
        ],
        out_specs=[
            pl.BlockSpec((tb, CCH), lambda b, t: (b * nt + t, 0)),
            pl.BlockSpec((None, CPAD, CCH), lambda b, t: (b, 0, 0)),
        ],
        out_shape=[
            jax.ShapeDtypeStruct((nseq * tpad, CCH), out_dtype),
            jax.ShapeDtypeStruct((nseq, CPAD, CCH), F32),
        ],
        scratch_shapes=[pltpu.VMEM((CPAD + tb, CCH), F32), pltpu.VMEM((7, tb + CPAD - 8, CCH), F32),
                        pltpu.VMEM((tb, CCH), F32)],
        compiler_params=_params(("arbitrary", "arbitrary")),
        name="conv",
    )(u, u, state_pad, dw_w, dw_b, ln_g, ln_b, pw2_bf)


def _gla_kernel(gq_ref, gk_ref, gv_ref, gg_ref, glr_ref, s0_ref, w2h_ref, w2l_ref, gkb_ref, ng_ref,
                ts_ref, gmat_ref, o_ref, sout_ref, s_ref, *, rows, t_real, n_steps):
    c = pl.program_id(1)
    nseq = gq_ref.shape[0]
    nch = max(rows // BLK, 1)
    crow = min(rows, BLK)
    items = [(g, ch) for g in range(nseq) for ch in range(nch)]
    idx = range(len(items))

    @pl.when(c == 0)
    def _():
        s_ref[...] = s0_ref[...]

    def chunk(ref, k):
        g, ch = items[k]
        x = ref[g, ch * BLK:ch * BLK + crow, :]
        if crow == BLK:
            return x
        return jnp.concatenate([x, jnp.zeros((BLK - crow, x.shape[1]), x.dtype)], axis=0)

    def rows_of(x, k):
        return x[k * BLK:(k + 1) * BLK]

    ridx = lax.broadcasted_iota(jnp.int32, (BLK, GKW), 0)
    valid_ch = [ridx < jnp.minimum(crow, t_real - c * rows - ch * BLK) for ch in range(nch)]
    valid = [valid_ch[ch] for _, ch in items]
    gq = [chunk(gq_ref, k) * (GDK ** -0.5) for k in idx]
    gk = [jnp.where(valid[k], chunk(gk_ref, k), 0.0) for k in idx]
    gv_bf = [chunk(gv_ref, k).astype(BF16) for k in idx]
    glr_h, glr_l = _split2(jnp.concatenate([chunk(glr_ref, k) for k in idx], axis=0))
    x = _dot(glr_h, w2h_ref[...]) + _dot(glr_l, w2h_ref[...]) + _dot(glr_h, w2l_ref[...]) + gkb_ref[...]
    lg_all = jnp.where(jnp.concatenate(valid, axis=0), _log_sigmoid(x) * (1.0 / GATE_NORM), 0.0)
    lg = [rows_of(lg_all, k) for k in idx]
    ts = ts_ref[...]
    lg_h, lg_m, lg_l = _split3(jnp.concatenate(lg, axis=1))
    br = _dot(ts, lg_h) + _dot(ts, lg_m) + _dot(ts, lg_l)
    b = [br[:BLK, k * GKW:(k + 1) * GKW] for k in idx]
    r = [br[BLK:, k * GKW:(k + 1) * GKW] for k in idx]
    qd = [gq[k] * jnp.exp(b[k] - r[k]) for k in idx]
    nst = GH * GSUB
    qmask = (lax.broadcasted_iota(jnp.int32, (nst, GKW), 1) // GDK) == (lax.broadcasted_iota(jnp.int32, (nst, GKW), 0) // GSUB)
    vmask = (lax.broadcasted_iota(jnp.int32, (nst, GVW), 1) // GDV) == (lax.broadcasted_iota(jnp.int32, (nst, GVW), 0) // GSUB)
    trow = lax.rem(lax.broadcasted_iota(jnp.int32, (nst, BLK), 0), GSUB)
    scol = lax.broadcasted_iota(jnp.int32, (nst, BLK), 1)
    nsc = BLK // GSUB
    att = [[] for _ in idx]
    for i in range(nsc):
        lo = i * GSUB
        kd = [(gk[k] * jnp.exp(r[k][lo:lo + 1, :] - b[k])).astype(BF16) for k in idx]
        qst = [jnp.where(qmask, jnp.concatenate([qd[k][lo:lo + GSUB]] * GH, axis=0), 0.0).astype(BF16) for k in idx]
        for k in idx:
            att[k].append(jnp.where(scol <= trow + lo, _dot_nt(qst[k], kd[k]), 0.0).astype(BF16))
    vmask_all = jnp.concatenate([vmask] * nsc, axis=0)
    ov = [jnp.where(vmask_all, _dot(jnp.concatenate(att[k], axis=0), gv_bf[k]), 0.0) for k in idx]
    kdec_t = [(gk[k] * jnp.exp(b[k][BLK - 1:BLK, :] - b[k])).T.astype(BF16) for k in idx]
    kv = [_dot(kdec_t[k], gv_bf[k]) for k in idx]
    bdmask = (lax.broadcasted_iota(jnp.int32, (GKW, GVW), 0) // GDK) == (lax.broadcasted_iota(jnp.int32, (GKW, GVW), 1) // GDV)
    s_cur = [s_ref[g] for g in range(nseq)]
    o_inter = [None] * len(items)
    for ch in range(nch):
        for g in range(nseq):
            k = g * nch + ch
            o_inter[k] = _dot((gq[k] * jnp.exp(b[k])).astype(BF16), s_cur[g].astype(BF16))
        for g in range(nseq):
            k = g * nch + ch
            bl_col = jnp.sum(lg[k].T, axis=1, keepdims=True)
            s_cur[g] = jnp.exp(bl_col) * s_cur[g] + jnp.where(bdmask, kv[k], 0.0)
    for g in range(nseq):
        s_ref[g] = s_cur[g]

    def heads_summed(x, i):
        parts = [x[(i * GH + h) * GSUB:(i * GH + h + 1) * GSUB] for h in range(GH)]
        return functools.reduce(lambda acc, part: acc + part, parts)

    o = jnp.concatenate([jnp.concatenate([heads_summed(ov[k], i) for i in range(nsc)], axis=0) + o_inter[k]
                         for k in idx], axis=0)
    o2_h, o2_l = _split2(o * o)
    ms = _dot(o2_h, gmat_ref[...]) + _dot(o2_l, gmat_ref[...])
    gate = jnp.concatenate([chunk(gg_ref, k) for k in idx], axis=0)
    res = o * lax.rsqrt(ms + EPS) * ng_ref[...] * _silu(gate)
    for k, (g, ch) in enumerate(items):
        o_ref[g, ch * BLK:ch * BLK + crow, :] = rows_of(res, k)[:crow].astype(o_ref.dtype)

    @pl.when(c == n_steps - 1)
    def _():
        sout_ref[...] = s_ref[...]


def _gla(u, ubase, s0_bd, w2h, w2l, gkb, ng_tiled, ts, gmat, layer, nseq, tpad, t_real, rows, group, out_dtype):
    nc = tpad // rows
    u3 = u.reshape(nseq, tpad, u.shape[1])
    kern = functools.partial(_gla_kernel, rows=rows, t_real=t_real, n_steps=nc)

    def cols(off, width):
        return pl.BlockSpec((group, rows, width), lambda b, c: (b, c, (off - ubase) // width))

    state_spec = pl.BlockSpec((group, GKW, GVW), lambda b, c: (b, 0, 0))
    o, s_out = pl.pallas_call(
        kern,
        grid=(nseq // group, nc),
        in_specs=[
            cols(OFF_GQ, GKW), cols(OFF_GK, GKW), cols(OFF_GV, GVW), cols(OFF_GG, GVW), cols(OFF_GLR, LANE),
            state_spec,
            pl.BlockSpec((None, LANE, GKW), lambda b, c: (layer, 0, 0)),
            pl.BlockSpec((None, LANE, GKW), lambda b, c: (layer, 0, 0)),
            pl.BlockSpec((None, 1, GKW), lambda b, c: (layer, 0, 0)),
            pl.BlockSpec((None, 1, GVW), lambda b, c: (layer, 0, 0)),
            pl.BlockSpec((2 * BLK, BLK), lambda b, c: (0, 0)),
            pl.BlockSpec((GVW, GVW), lambda b, c: (0, 0)),
        ],
        out_specs=[
            pl.BlockSpec((group, rows, GVW), lambda b, c: (b, c, 0)),
            state_spec,
        ],
        out_shape=[
            jax.ShapeDtypeStruct((nseq, tpad, GVW), out_dtype),
            jax.ShapeDtypeStruct((nseq, GKW, GVW), F32),
        ],
        scratch_shapes=[pltpu.VMEM((group, GKW, GVW), F32)],
        compiler_params=_params(("arbitrary", "arbitrary")),
        name="gla",
    )(u3, u3, u3, u3, u3, s0_bd, w2h, w2l, gkb, ng_tiled, ts, gmat)
    return o.reshape(nseq * tpad, GVW), s_out


def _state_to_blockdiag(s):
    n = s.shape[0]
    out = jnp.zeros((n, GH, GDK, GH, GDV), F32)
    for h in range(GH):
        out = out.at[:, h, :, h, :].set(s[:, h].astype(F32))
    return out.reshape(n, GKW, GVW)


def _blockdiag_to_state(s_bd):
    n = s_bd.shape[0]
    s5 = s_bd.reshape(n, GH, GDK, GH, GDV)
    return jnp.stack([s5[:, h, :, h, :] for h in range(GH)], axis=1)


def _row_block(n):
    for bm in (512, 256, 128, 64, 32, 16, 8):
        if n % bm == 0:
            return bm
    raise ValueError(f"row count {n} is not a multiple of 8")


def kernel(x_prompt, x_sample, cache_k, cache_v, state_conv, state_gla, page_table, meta_tokens, norm_g, w_in,
           sb_bias, conv_dw_w, conv_dw_b, conv_ln_g, conv_ln_b, conv_pw2, gla_gk_w2, gla_gk_b, gla_norm_g, w_out,
           final_norm_g):
    bp, seq, _ = x_prompt.shape
    bd, dseq, _ = x_sample.shape
    depth = w_in.shape[0]
    n_pool, page = cache_k.shape[1], cache_k.shape[2]
    assert page == BLK and dseq % 8 == 0 and BLK % dseq == 0
    t_real = N_META + seq
    tpad = -(-t_real // KBLK) * KBLK

    w_in_bf = jnp.pad(w_in, ((0, 0), (0, 0), (0, IN_WP - IN_W))).astype(BF16)
    wkv_t = jnp.transpose(w_in[:, :, OFF_K:OFF_SBG], (0, 2, 1)).astype(BF16)
    w_out_bf = w_out.astype(BF16)
    norm_g3 = norm_g.reshape(depth, 1, D_MODEL)
    dw_w = jnp.repeat(conv_dw_w, 8, axis=1)
    dw_b = conv_dw_b.reshape(depth, 1, CCH)
    ln_g = conv_ln_g.reshape(depth, 1, CCH)
    ln_b = conv_ln_b.reshape(depth, 1, CCH)
    pw2_bf = conv_pw2.astype(BF16)
    w2 = jnp.pad(gla_gk_w2, ((0, 0), (0, LANE - GRANK), (0, 0)))
    w2h = w2.astype(BF16)
    w2l = (w2 - w2h.astype(F32)).astype(BF16)
    gkb = gla_gk_b.reshape(depth, 1, GKW)
    ng_tiled = jnp.tile(gla_norm_g, (1, GH)).reshape(depth, 1, GVW)
    ii = jnp.arange(BLK)
    uo = jnp.concatenate([(ii[:, None] > ii[None, :]), jnp.ones((BLK, LANE), bool)], axis=1).astype(BF16)
    ik = jnp.arange(KBLK)
    uo_p = (ik[:, None] > ik[None, :]).astype(BF16)
    tri = ii[None, :] <= ii[:, None]
    sel = ii[None, :] < (ii[:, None] // GSUB) * GSUB
    ts = jnp.concatenate([tri, sel], axis=0).astype(BF16)
    gi = jnp.arange(GVW) // GDV
    gmat = ((gi[:, None] == gi[None, :]).astype(F32) / GDV).astype(BF16)
    final_g = final_norm_g.reshape(1, D_MODEL)

    meta = jnp.broadcast_to(meta_tokens[None].astype(F32), (bp, N_META, D_MODEL))
    xp = jnp.concatenate([meta, x_prompt, jnp.zeros((bp, tpad - t_real, D_MODEL), F32)], axis=1)
    xp = xp.reshape(bp * tpad, D_MODEL)
    xs = x_sample.reshape(bd * dseq, D_MODEL)
    bm_p = _row_block(bp * tpad)
    bm_s = _row_block(bd * dseq)
    conv_tb = KBLK if tpad % KBLK == 0 else BLK
    nsub = max(d for d in (2, 4, 6) if (tpad // BLK) % d == 0)
    group = max(d for d in range(1, 17) if bd % d == 0)
    gla_group_p = max(d for d in range(1, 5) if bp % d == 0)
    gla_group_s = max(d for d in range(1, 5) if bd % d == 0)
    cache_k4 = jnp.transpose(cache_k, (0, 1, 3, 4, 2)).reshape(depth, n_pool, SBW, page)
    cache_v4 = jnp.transpose(cache_v, (0, 1, 3, 4, 2)).reshape(depth, n_pool, SBW, page)
    zero_conv = jnp.zeros((bp, CPAD, CCH), F32)
    zero_gla = jnp.zeros((bp, GKW, GVW), F32)

    cp_l, sp_l, ks_l, vs_l, cs_l, ss_l = ([] for _ in range(6))
    kt_all = vt_all = None
    for l in range(depth):
        bias2 = sb_bias[l] * LOG2E
        bias_rows_p = jnp.tile(jnp.repeat(bias2.reshape(NH // 2, 2), BLK, axis=1), (1, nsub))
        bias_pairs = jnp.pad(jnp.stack(_split3(bias_rows_p), axis=-1), ((0, 0), (0, 0), (0, LANE - N_BIAS_PIECES)))
        bias_rows = jnp.broadcast_to(jnp.repeat(bias2, dseq)[:, None], (NH * dseq, page))
        last = l == depth - 1
        u_p, q_p, kt_all, vt_all, kvt_p = _inproj_prompt(xp, norm_g3, w_in_bf, wkv_t, kt_all, vt_all, l, bp, tpad,
                                                         t_real, depth, KBLK)
        u_s = _inproj(xs, norm_g3, w_in_bf, l, bm_s)
        mix_sb, smix_sb = _attn(q_p, kvt_p, u_p, OFF_SBG, bias_pairs, uo_p, bp, tpad, nsub, u_s, cache_k4, cache_v4,
                                page_table, bias_rows, uo, l, bd, dseq, group)
        mix_cv, cst_p = _conv(u_p, OFF_SBG, zero_conv, dw_w, dw_b, ln_g, ln_b, pw2_bf, l, bp, tpad, t_real, conv_tb,
                              BF16)
        mix_gla, gst_p = _gla(u_p, OFF_SBG, zero_gla, w2h, w2l, gkb, ng_tiled, ts, gmat, l, bp, tpad, t_real, KBLK,
                              gla_group_p, BF16)
        xp = _outproj(xp, mix_sb, mix_cv, mix_gla, w_out_bf, l, bm_p, final_g if last else None)
        cp_l.append(cst_p[:, CPAD - (CW - 1):])
        sp_l.append(_blockdiag_to_state(gst_p))
        st_s = jnp.pad(state_conv[l], ((0, 0), (CPAD - (CW - 1), 0), (0, 0)))
        smix_cv, cst_s = _conv(u_s, 0, st_s, dw_w, dw_b, ln_g, ln_b, pw2_bf, l, bd, dseq, dseq, dseq, F32)
        smix_gla, gst_s = _gla(u_s, 0, _state_to_blockdiag(state_gla[l]), w2h, w2l, gkb, ng_tiled, ts, gmat, l, bd,
                               dseq, dseq, dseq, gla_group_s, F32)
        xs = _outproj(xs, smix_sb, smix_cv, smix_gla, w_out_bf, l, bm_s, final_g if last else None)
        ks_l.append(u_s[:, OFF_K:OFF_V].reshape(bd, dseq, NH, DH))
        vs_l.append(u_s[:, OFF_V:OFF_SBG].reshape(bd, dseq, NH, DH))
        cs_l.append(cst_s[:, CPAD - (CW - 1):])
        ss_l.append(_blockdiag_to_state(gst_s))

    y_prompt = xp.reshape(bp, tpad, D_MODEL)[:, N_META:t_real]
    y_sample = xs.reshape(bd, dseq, D_MODEL)
    new_k_prompt = jnp.transpose(kt_all[..., :t_real].reshape(depth, bp, NH, DH, t_real), (0, 1, 4, 2, 3))
    new_v_prompt = jnp.transpose(vt_all[..., :t_real].reshape(depth, bp, NH, DH, t_real), (0, 1, 4, 2, 3))
    return (y_prompt, y_sample, new_k_prompt, new_v_prompt, jnp.stack(cp_l), jnp.stack(sp_l),
            jnp.stack(ks_l), jnp.stack(vs_l), jnp.stack(cs_l), jnp.stack(ss_l))
```
